```python
import math
import jax, jax.numpy as jnp
from jax import lax
import numpy as np

D_MODEL = 1024
BATCH = 16
SEQ = 4096
DEPTH = 1

DIFF_HEADS = D_MODEL // 256
DIFF_QK_DIM = 64
DIFF_V_DIM = 2 * DIFF_QK_DIM
DIFF_WIDTH = DIFF_HEADS * DIFF_V_DIM
MOBA_HEADS = D_MODEL // 128
MOBA_HEAD_DIM = 64
MOBA_WIDTH = MOBA_HEADS * MOBA_HEAD_DIM
MOBA_BLOCK = 256
MOBA_TOPK = 3
MIX_WIDTH = DIFF_WIDTH + MOBA_WIDTH
Q_BLOCK = 128
DIFF_QK_COLS = DIFF_HEADS * 2 * DIFF_QK_DIM
IN_COLS = 2 * DIFF_QK_COLS + DIFF_WIDTH + 3 * MOBA_WIDTH
REL_BUCKETS = 32
REL_MAX_DIST = 128
N_BIAS_COLS = 2 * DIFF_HEADS + MOBA_HEADS
PEER_HEADS = 8
PEER_NKEYS = 128
PEER_N_EXPERTS = PEER_NKEYS * PEER_NKEYS
PEER_KEY_DIM = 128
PEER_TOPK = 16
PEER_CHUNK = 128
NORM_EPS = 1e-6
NEG_INF = -1e30

kernel_name = 'hymba_diffattn_moba_peer_block'


def rmsnorm(x, g):
    xf = x.astype(jnp.float32)
    y = xf * lax.rsqrt(jnp.mean(xf * xf, axis=-1, keepdims=True) + NORM_EPS)
    return (y * g.astype(jnp.float32)).astype(x.dtype)


def t5_bucket(n):
    max_exact = REL_BUCKETS // 2
    nf = jnp.maximum(n, 1).astype(jnp.float32)
    large = max_exact + (jnp.log(nf / max_exact) / math.log(REL_MAX_DIST / max_exact)
                         * (REL_BUCKETS - max_exact)).astype(jnp.int32)
    large = jnp.minimum(large, REL_BUCKETS - 1)
    return jnp.where(n < max_exact, n, large)


def diff_attention(q, k, v, lam, bias_t, lambda_init, subln_g):
    B, S = q.shape[0], q.shape[1]
    H = DIFF_HEADS
    nqb = S // Q_BLOCK
    scale = DIFF_QK_DIM ** -0.5
    qb = q.reshape(B, nqb, Q_BLOCK, H, 2, DIFF_QK_DIM).transpose(1, 0, 3, 4, 2, 5)
    kt = k.transpose(0, 2, 3, 1, 4)
    vt = v.transpose(0, 2, 1, 3)
    kpos = jnp.arange(S)

    def one_block(args):
        qi, i = args
        q_pos = i * Q_BLOCK + jnp.arange(Q_BLOCK)
        dist = q_pos[:, None] - kpos[None, :]
        bias = bias_t[:, :, jnp.clip(dist, 0, S - 1)].astype(jnp.float32)
        logits = jnp.einsum('bhmqd,bhmkd->bhmqk', qi, kt).astype(jnp.float32) * scale + bias[None]
        logits = jnp.where(dist >= 0, logits, NEG_INF)
        p = jax.nn.softmax(logits, axis=-1)
        attn = (p[:, :, 0] - lam * p[:, :, 1]).astype(vt.dtype)
        return jnp.einsum('bhqk,bhkd->bhqd', attn, vt)

    out = lax.map(one_block, (qb, jnp.arange(nqb)))
    out = out.transpose(1, 0, 3, 2, 4).reshape(B, S, H, DIFF_V_DIM)
    out = rmsnorm(out, subln_g) * (1.0 - lambda_init)
    return out.reshape(B, S, H * DIFF_V_DIM)


def moba_attention(q, k, v, bias_t):
    B, S = q.shape[0], q.shape[1]
    H, dh = MOBA_HEADS, MOBA_HEAD_DIM
    nblk = -(-S // MOBA_BLOCK)
    S_pad = nblk * MOBA_BLOCK
    nqb = S // Q_BLOCK
    scale = dh ** -0.5
    qt = q.transpose(0, 2, 1, 3)
    pad = ((0, 0), (0, 0), (0, S_pad - S), (0, 0))
    kb = jnp.pad(k.transpose(0, 2, 1, 3), pad).reshape(B, H, nblk, MOBA_BLOCK, dh)
    vb = jnp.pad(v.transpose(0, 2, 1, 3), pad).reshape(B, H, nblk, MOBA_BLOCK, dh)
    kmean = kb.astype(jnp.float32).mean(axis=3)
    gate = jnp.einsum('bhsd,bhnd->bhsn', qt.astype(jnp.float32), kmean)
    past = jnp.arange(nblk)[None, :] < (jnp.arange(S) // MOBA_BLOCK)[:, None]
    gate = jnp.where(past, gate, NEG_INF)
    ksel = min(MOBA_TOPK, nblk)
    _, sel = lax.top_k(gate, ksel)

    q_items = qt.reshape(B, H, nqb, Q_BLOCK, dh).transpose(0, 2, 1, 3, 4).reshape(B * nqb, H, Q_BLOCK, dh)
    sel_items = sel.reshape(B, H, nqb, Q_BLOCK, ksel).transpose(0, 2, 1, 3, 4).reshape(B * nqb, H, Q_BLOCK, ksel)
    b_ids = jnp.repeat(jnp.arange(B), nqb)
    qb_ids = jnp.tile(jnp.arange(nqb), B)
    head_ids = jnp.arange(H)
    offs = jnp.arange(MOBA_BLOCK)

    def one_item(args):
        qi, si, b, i = args
        kbb, vbb = kb[b], vb[b]
        q_pos = i * Q_BLOCK + jnp.arange(Q_BLOCK)
        own = (i * Q_BLOCK) // MOBA_BLOCK
        k_own = lax.dynamic_index_in_dim(kbb, own, axis=1, keepdims=False)
        v_own = lax.dynamic_index_in_dim(vbb, own, axis=1, keepdims=False)
        k_sel = kbb[head_ids[:, None, None], si]
        v_sel = vbb[head_ids[:, None, None], si]
        dist_sel = q_pos[None, :, None, None] - (si[..., None] * MOBA_BLOCK + offs)
        bias_sel = bias_t[head_ids[:, None, None, None], jnp.clip(dist_sel, 0, S_pad - 1)].astype(jnp.float32)
        l_sel = jnp.einsum('hqd,hqjpd->hqjp', qi, k_sel).astype(jnp.float32) * scale + bias_sel
        l_sel = jnp.where((si < own)[..., None], l_sel, NEG_INF)
        dist_own = q_pos[:, None] - (own * MOBA_BLOCK + offs)[None, :]
        bias_own = bias_t[:, jnp.clip(dist_own, 0, S_pad - 1)].astype(jnp.float32)
        l_own = jnp.einsum('hqd,hpd->hqp', qi, k_own).astype(jnp.float32) * scale + bias_own
        l_own = jnp.where(dist_own >= 0, l_own, NEG_INF)
        logits = jnp.concatenate([l_sel.reshape(H, Q_BLOCK, ksel * MOBA_BLOCK), l_own], axis=-1)
        p = jax.nn.softmax(logits, axis=-1).astype(v.dtype)
        p_sel = p[..., :ksel * MOBA_BLOCK].reshape(H, Q_BLOCK, ksel, MOBA_BLOCK)
        p_own = p[..., ksel * MOBA_BLOCK:]
        return (jnp.einsum('hqjp,hqjpd->hqd', p_sel, v_sel)
                + jnp.einsum('hqp,hpd->hqd', p_own, v_own))

    out = lax.map(one_item, (q_items, sel_items, b_ids, qb_ids))
    out = out.reshape(B, nqb, H, Q_BLOCK, dh).transpose(0, 1, 3, 2, 4)
    return out.reshape(B, S, H * dh)


def peer(h, wq, sub_keys, exp_u, exp_v):
    B, S, D = h.shape
    tokens = h.reshape(-1, PEER_CHUNK, D)

    def chunk(hc):
        T = hc.shape[0]
        q = (hc @ wq).reshape(T, PEER_HEADS, 2, PEER_KEY_DIM)
        s1 = jnp.einsum('thd,hnd->thn', q[:, :, 0], sub_keys[:, 0]).astype(jnp.float32)
        s2 = jnp.einsum('thd,hnd->thn', q[:, :, 1], sub_keys[:, 1]).astype(jnp.float32)
        v1, i1 = lax.top_k(s1, PEER_TOPK)
        v2, i2 = lax.top_k(s2, PEER_TOPK)
        cand = (v1[..., :, None] + v2[..., None, :]).reshape(T, PEER_HEADS, PEER_TOPK * PEER_TOPK)
        sc, ci = lax.top_k(cand, PEER_TOPK)
        e = (jnp.take_along_axis(i1, ci // PEER_TOPK, axis=-1) * PEER_NKEYS
             + jnp.take_along_axis(i2, ci % PEER_TOPK, axis=-1))
        g = jax.nn.softmax(sc, axis=-1)
        u_g = exp_u[e]
        act = jax.nn.gelu(jnp.einsum('td,thkd->thk', hc, u_g).astype(jnp.float32), approximate=False)
        w = (g * act).astype(hc.dtype)
        return jnp.einsum('thk,thkd->td', w, exp_v[e])

    return lax.map(chunk, tokens).reshape(B, S, D)


def setup_inputs(seed: int = 0) -> dict:
    key = jax.random.key(seed)
    ks = jax.random.split(key, 18)
    D = D_MODEL
    f32 = jnp.float32
    nrm = lambda k, shape, s: jax.random.normal(k, shape, f32) * s
    return {
        'x': nrm(ks[0], (BATCH, SEQ, D), 1.0),
        'c': nrm(ks[1], (BATCH, D), 1.0),
        'rel_bias': nrm(ks[2], (REL_BUCKETS, N_BIAS_COLS), 0.5),
        'w_ada': nrm(ks[3], (DEPTH, D, 6 * D), D ** -0.5),
        'b_ada': nrm(ks[4], (DEPTH, 6 * D), 0.02),
        'norm1_g': 1.0 + nrm(ks[5], (DEPTH, D), 0.05),
        'w_in': nrm(ks[6], (DEPTH, D, IN_COLS), D ** -0.5),
        'diff_lambda': nrm(ks[7], (DEPTH, 4, DIFF_QK_DIM), 0.1),
        'diff_subln_g': 1.0 + nrm(ks[8], (DEPTH, DIFF_V_DIM), 0.05),
        'w_out': nrm(ks[9], (DEPTH, MIX_WIDTH, D), MIX_WIDTH ** -0.5),
        'norm2_g': 1.0 + nrm(ks[10], (DEPTH, D), 0.05),
        'peer_wq': nrm(ks[11], (DEPTH, D, PEER_HEADS * 2 * PEER_KEY_DIM), D ** -0.5),
        'peer_keys': nrm(ks[12], (DEPTH, PEER_HEADS, 2, PEER_NKEYS, PEER_KEY_DIM), PEER_KEY_DIM ** -0.5),
        'peer_u': nrm(ks[13], (DEPTH, PEER_N_EXPERTS, D), D ** -0.5),
        'peer_v': nrm(ks[14], (DEPTH, PEER_N_EXPERTS, D), PEER_HEADS ** -0.5),
        'final_g': 1.0 + nrm(ks[15], (D,), 0.05),
    }


def reference(x, c, rel_bias, w_ada, b_ada, norm1_g, w_in, diff_lambda, diff_subln_g,
              w_out, norm2_g, peer_wq, peer_keys, peer_u, peer_v, final_g):
    B, S, D = x.shape
    S_pad = -(-S // MOBA_BLOCK) * MOBA_BLOCK
    bias_by_dist = rel_bias[t5_bucket(jnp.arange(S_pad))]
    diff_bias_t = bias_by_dist[:, :2 * DIFF_HEADS].reshape(S_pad, 2, DIFF_HEADS).transpose(2, 1, 0)
    moba_bias_t = bias_by_dist[:, 2 * DIFF_HEADS:].T
    c_act = jax.nn.silu(c)
    splits = np.cumsum([DIFF_QK_COLS, DIFF_QK_COLS, DIFF_WIDTH, MOBA_WIDTH, MOBA_WIDTH]).tolist()
    for l in range(DEPTH):
        mod = c_act @ w_ada[l] + b_ada[l]
        sh1, sc1, g1, sh2, sc2, g2 = jnp.split(mod, 6, axis=-1)
        h = rmsnorm(x, norm1_g[l]) * (1.0 + sc1[:, None]) + sh1[:, None]
        proj = h @ w_in[l]
        dq, dk, dv, mq, mk, mv = jnp.split(proj, splits, axis=-1)
        lambda_init = 0.8 - 0.6 * math.exp(-0.3 * l)
        lp = diff_lambda[l].astype(jnp.float32)
        lam = jnp.exp(jnp.sum(lp[0] * lp[1])) - jnp.exp(jnp.sum(lp[2] * lp[3])) + lambda_init
        d_out = diff_attention(
            dq.reshape(B, S, DIFF_HEADS, 2, DIFF_QK_DIM),
            dk.reshape(B, S, DIFF_HEADS, 2, DIFF_QK_DIM),
            dv.reshape(B, S, DIFF_HEADS, DIFF_V_DIM),
            lam, diff_bias_t, lambda_init, diff_subln_g[l])
        m_out = moba_attention(
            mq.reshape(B, S, MOBA_HEADS, MOBA_HEAD_DIM),
            mk.reshape(B, S, MOBA_HEADS, MOBA_HEAD_DIM),
            mv.reshape(B, S, MOBA_HEADS, MOBA_HEAD_DIM),
            moba_bias_t)
        mix = jnp.concatenate([d_out, m_out], axis=-1) @ w_out[l]
        x = x + g1[:, None] * mix
        h2 = rmsnorm(x, norm2_g[l]) * (1.0 + sc2[:, None]) + sh2[:, None]
        x = x + g2[:, None] * peer(h2, peer_wq[l], peer_keys[l], peer_u[l], peer_v[l])
    return rmsnorm(x, final_g)
```

```python
import functools
import math

import jax
import jax.numpy as jnp
from jax import lax
from jax.experimental import pallas as pl
from jax.experimental.pallas import tpu as pltpu

F32 = jnp.float32
BF16 = jnp.bfloat16

D_MODEL = 1024
DIFF_HEADS = 4
MOBA_HEADS = 8
HEAD_COLS = 128
HALF = 64
MOBA_BLOCK = 256
MOBA_TOPK = 3
IN_COLS = 3072
REL_BUCKETS = 32
REL_MAX_DIST = 128
PEER_HEADS = 8
PEER_NKEYS = 128
PEER_KEY_DIM = 128
PEER_TOPK = 16
NORM_EPS = 1e-6
NEG_INF = -1e30
LAMBDA_INIT = 0.8 - 0.6 * math.exp(-0.3 * 0)

VMEM_LIMIT_BYTES = 56 * 1024 * 1024


def _params(semantics):
    return pltpu.CompilerParams(dimension_semantics=semantics, vmem_limit_bytes=VMEM_LIMIT_BYTES)


def _mod_kernel(c_ref, w_ref, b_ref, o_ref):
    c = c_ref[...]
    act = c * jax.nn.sigmoid(c)
    o_ref[...] = jnp.dot(act, w_ref[...], preferred_element_type=F32,
                         precision=lax.Precision.HIGHEST) + b_ref[...]


def _modulation(c, w_ada, b_ada):
    B, D = c.shape
    n_out = w_ada.shape[1]
    tn = 1024
    return pl.pallas_call(
        _mod_kernel,
        grid=(n_out // tn,),
        in_specs=[pl.BlockSpec((B, D), lambda n: (0, 0)),
                  pl.BlockSpec((D, tn), lambda n: (0, n)),
                  pl.BlockSpec((1, tn), lambda n: (0, n))],
        out_specs=pl.BlockSpec((B, tn), lambda n: (0, n)),
        out_shape=jax.ShapeDtypeStruct((B, n_out), F32),
        compiler_params=_params(("arbitrary",)),
        name="mod",
    )(c, w_ada, b_ada.reshape(1, n_out))


def _rms_modulate(x, g, sc, sh):
    ms = jnp.mean(x * x, axis=-1, keepdims=True)
    return (x * lax.rsqrt(ms + NORM_EPS) * g) * (1.0 + sc) + sh


def _in_kernel(x_ref, g_ref, sc_ref, sh_ref, w_ref, proj_ref, kmean_ref, *, tm, cn, mk_chunk):
    h = _rms_modulate(x_ref[0], g_ref[...], sc_ref[0], sh_ref[0]).astype(BF16)
    for n in range(IN_COLS // cn):
        acc = jnp.dot(h, w_ref[:, n * cn:(n + 1) * cn], preferred_element_type=F32)
        proj_ref[0, :, n * cn:(n + 1) * cn] = acc.astype(BF16)
        if n == mk_chunk:
            for r in range(tm // MOBA_BLOCK):
                kmean_ref[0, r] = jnp.mean(acc[r * MOBA_BLOCK:(r + 1) * MOBA_BLOCK], axis=0, keepdims=True)


def _in_proj(x, g, sc, sh, w_in_bf16, tm=512):
    B, S, D = x.shape
    cn = 512
    mk_chunk = 4
    nblk = S // MOBA_BLOCK
    kern = functools.partial(_in_kernel, tm=tm, cn=cn, mk_chunk=mk_chunk)
    vec = pl.BlockSpec((1, 1, D), lambda b, i: (b, 0, 0))
    return pl.pallas_call(
        kern,
        grid=(B, S // tm),
        in_specs=[pl.BlockSpec((1, tm, D), lambda b, i: (b, i, 0)),
                  pl.BlockSpec((1, D), lambda b, i: (0, 0)),
                  vec, vec,
                  pl.BlockSpec((D, IN_COLS), lambda b, i: (0, 0))],
        out_specs=[pl.BlockSpec((1, tm, IN_COLS), lambda b, i: (b, i, 0)),
                   pl.BlockSpec((1, tm // MOBA_BLOCK, 1, 512), lambda b, i: (b, i, 0, 0))],
        out_shape=[jax.ShapeDtypeStruct((B, S, IN_COLS), BF16),
                   jax.ShapeDtypeStruct((B, nblk, 1, 512), F32)],
        compiler_params=_params(("parallel", "arbitrary")),
        name="in_proj",
    )(x, g.reshape(1, D), sc.reshape(B, 1, D), sh.reshape(B, 1, D), w_in_bf16)


def _attn_kernel(*refs, moba, tq, tk, nblk):
    if moba:
        q_ref, k_ref, v_ref, bias_ref, kmt_ref, o_ref, q2_s, m_s, l_s, acc_s, selb_s = refs
    else:
        q_ref, k_ref, v_ref, bias_ref, lam_ref, g_ref, o_ref, q2_s, m_s, l_s, acc_s = refs
    i = pl.program_id(2)
    lane = lax.broadcasted_iota(jnp.int32, (tq, HEAD_COLS), 1)
    reps = tk // HEAD_COLS

    qs = q_ref[0] * 0.125
    q2_s[0:tq] = jnp.where(lane < HALF, qs, jnp.zeros_like(qs))
    q2_s[tq:2 * tq] = jnp.where(lane >= HALF, qs, jnp.zeros_like(qs))
    m_s[...] = jnp.full(m_s.shape, -jnp.inf, F32)
    l_s[...] = jnp.zeros(l_s.shape, F32)
    acc_s[...] = jnp.zeros(acc_s.shape, F32)
    if moba:
        gate = jnp.dot(q2_s[...].astype(F32), kmt_ref[0], preferred_element_type=F32,
                       precision=lax.Precision.HIGHEST)
        lane2 = lax.broadcasted_iota(jnp.int32, (2 * tq, HEAD_COLS), 1)
        cnt = jnp.zeros((2 * tq, HEAD_COLS), jnp.int32)
        for jp in range(nblk):
            col = gate[:, jp:jp + 1]
            beats = (col > gate) | ((col == gate) & (lane2 > jp))
            cnt = cnt + jnp.where(beats, jnp.where(i > jp, 1, 0), 0)
        sel = ((lane2 < i) & (cnt < MOBA_TOPK)) | (lane2 == i)
        sb = jnp.where(sel, 0.0, NEG_INF).astype(F32)
        for jb in range(nblk):
            selb_s[jb] = jnp.broadcast_to(sb[:, jb:jb + 1], (2 * tq, HEAD_COLS))

    def step(j, carry):
        rows = pl.ds(pl.multiple_of(j * tk, tk), tk)
        s = lax.dot_general(q2_s[...], k_ref[0, rows, :], (((1,), (1,)), ((), ())),
                            preferred_element_type=F32)
        kind = jnp.minimum(i - j, 2)
        ps = []
        for m in range(2):
            sm = s[m * tq:(m + 1) * tq] + bias_ref[0, m, kind]
            if moba:
                sm = sm + jnp.concatenate([selb_s[j, m * tq:(m + 1) * tq]] * reps, axis=1)
            m_prev = m_s[m]
            m_new = jnp.maximum(m_prev, jnp.max(sm, axis=1, keepdims=True))
            alpha = jnp.exp(m_prev - m_new)
            p = jnp.exp(sm - jnp.concatenate([m_new] * reps, axis=1))
            l_s[m] = alpha * l_s[m] + jnp.sum(p, axis=1, keepdims=True)
            m_s[m] = m_new
            acc_s[m] = alpha * acc_s[m]
            ps.append(p.astype(BF16))
        pv = jnp.dot(jnp.concatenate(ps, axis=0), v_ref[0, rows, :], preferred_element_type=F32)
        acc_s[0] = acc_s[0] + pv[0:tq]
        acc_s[1] = acc_s[1] + pv[tq:2 * tq]
        return carry

    lax.fori_loop(0, i + 1, step, 0)

    o0 = acc_s[0] / l_s[0]
    o1 = acc_s[1] / l_s[1]
    if moba:
        o = jnp.where(lane < HALF, o0, o1)
    else:
        lp = lam_ref[...]
        lam = (jnp.exp(jnp.sum(lp[0:1] * lp[1:2], axis=1, keepdims=True))
               - jnp.exp(jnp.sum(lp[2:3] * lp[3:4], axis=1, keepdims=True)) + LAMBDA_INIT)
        d = o0 - lam * o1
        ms = jnp.mean(d * d, axis=-1, keepdims=True)
        o = (d * lax.rsqrt(ms + NORM_EPS) * g_ref[...]) * (1.0 - LAMBDA_INIT)
    o_ref[0] = o.astype(BF16)


def _attention(proj, bias_tiles, extra, *, moba, col0):
    B, S, _ = proj.shape
    tq = tk = MOBA_BLOCK
    nq = S // tq
    nblk = S // MOBA_BLOCK
    kern = functools.partial(_attn_kernel, moba=moba, tq=tq, tk=tk, nblk=nblk)
    in_specs = [
        pl.BlockSpec((1, tq, HEAD_COLS), lambda b, h, i: (b, i, col0 + h)),
        pl.BlockSpec((1, S, HEAD_COLS), lambda b, h, i: (b, 0, col0 + 4 + h)),
        pl.BlockSpec((1, S, HEAD_COLS), lambda b, h, i: (b, 0, col0 + 8 + h)),
        pl.BlockSpec((1, 2, 3, tq, tk), lambda b, h, i: (h, 0, 0, 0, 0)),
    ]
    scratch = [pltpu.VMEM((2 * tq, HEAD_COLS), BF16),
               pltpu.VMEM((2, tq, HEAD_COLS), F32),
               pltpu.VMEM((2, tq, HEAD_COLS), F32),
               pltpu.VMEM((2, tq, HEAD_COLS), F32)]
    if moba:
        (kmt,) = extra
        in_specs.append(pl.BlockSpec((1, HEAD_COLS, HEAD_COLS), lambda b, h, i: (b, h, 0)))
        scratch.append(pltpu.VMEM((nblk, 2 * tq, HEAD_COLS), F32))
        args = (proj, proj, proj, bias_tiles, kmt)
    else:
        lam_p, subln_g = extra
        in_specs.append(pl.BlockSpec(lam_p.shape, lambda b, h, i: (0, 0)))
        in_specs.append(pl.BlockSpec((1, HEAD_COLS), lambda b, h, i: (0, 0)))
        args = (proj, proj, proj, bias_tiles, lam_p, subln_g.reshape(1, HEAD_COLS))
    return pl.pallas_call(
        kern,
        grid=(B, 4, nq),
        in_specs=in_specs,
        out_specs=pl.BlockSpec((1, tq, HEAD_COLS), lambda b, h, i: (b, i, h)),
        out_shape=jax.ShapeDtypeStruct((B, S, 4 * HEAD_COLS), BF16),
        scratch_shapes=scratch,
        compiler_params=_params(("parallel", "parallel", "arbitrary")),
        name="moba_attn" if moba else "diff_attn",
    )(*args)


def _t5_bucket(n):
    max_exact = REL_BUCKETS // 2
    nf = jnp.maximum(n, 1).astype(F32)
    large = max_exact + (jnp.log(nf / max_exact) / math.log(REL_MAX_DIST / max_exact)
                         * (REL_BUCKETS - max_exact)).astype(jnp.int32)
    large = jnp.minimum(large, REL_BUCKETS - 1)
    return jnp.where(n < max_exact, n, large)


def _bias_tiles(rel_bias, S, t):
    by_dist = rel_bias[_t5_bucket(jnp.arange(S))]
    r = jnp.arange(t)[:, None]
    c = jnp.arange(t)[None, :]
    tiles = []
    for kind in range(3):
        dist = kind * t + r - c
        tile = by_dist[jnp.clip(dist, 0, S - 1)]
        if kind == 0:
            tile = jnp.where((dist >= 0)[..., None], tile, NEG_INF)
        tiles.append(tile)
    tiles = jnp.stack(tiles, axis=0).transpose(3, 0, 1, 2)
    diff = tiles[:2 * DIFF_HEADS].reshape(2, DIFF_HEADS, 3, t, t).transpose(1, 0, 2, 3, 4)
    moba = tiles[2 * DIFF_HEADS:].reshape(MOBA_HEADS // 2, 2, 3, t, t)
    return diff, moba


def _out_kernel(d_ref, m_ref, wd_ref, wm_ref, x_ref, g1_ref, ng_ref, sc_ref, sh_ref, x1_ref, h2_ref):
    mix = (jnp.dot(d_ref[0], wd_ref[...], preferred_element_type=F32)
           + jnp.dot(m_ref[0], wm_ref[...], preferred_element_type=F32))
    x1 = x_ref[0] + g1_ref[0] * mix
    x1_ref[0] = x1
    h2_ref[0] = _rms_modulate(x1, ng_ref[...], sc_ref[0], sh_ref[0]).astype(BF16)


def _out_proj(d_out, m_out, w_out_bf16, x, g1, norm_g, sc, sh, tm=512):
    B, S, D = x.shape
    half = d_out.shape[-1]
    vec = pl.BlockSpec((1, 1, D), lambda b, i: (b, 0, 0))
    tok = lambda w: pl.BlockSpec((1, tm, w), lambda b, i: (b, i, 0))
    return pl.pallas_call(
        _out_kernel,
        grid=(B, S // tm),
        in_specs=[tok(half), tok(half),
                  pl.BlockSpec((half, D), lambda b, i: (0, 0)),
                  pl.BlockSpec((half, D), lambda b, i: (1, 0)),
                  tok(D), vec,
                  pl.BlockSpec((1, D), lambda b, i: (0, 0)),
                  vec, vec],
        out_specs=[tok(D), tok(D)],
        out_shape=[jax.ShapeDtypeStruct((B, S, D), F32), jax.ShapeDtypeStruct((B, S, D), BF16)],
        compiler_params=_params(("parallel", "arbitrary")),
        name="out_proj",
    )(d_out, m_out, w_out_bf16, w_out_bf16, x, g1.reshape(B, 1, D), norm_g.reshape(1, D),
      sc.reshape(B, 1, D), sh.reshape(B, 1, D))


def _top_values(w, k):
    row = lax.broadcasted_iota(jnp.int32, (k, w.shape[1]), 0)
    out = jnp.zeros((k, w.shape[1]), F32)
    for r in range(k):
        m = jnp.max(w, axis=0, keepdims=True)
        out = jnp.where(row == r, m, out)
        w = jnp.where(w == m, -jnp.inf, w)
    return out


def _gelu_exact(a):
    return 0.5 * a * (1.0 + lax.erf(a * math.sqrt(0.5)))


def _peer_kernel(h2_ref, x1_ref, g2_ref, wqt_ref, keys_ref, u_ref, vt_ref, fg_ref, o_ref,
                 s_s, e_s, tau_s, acc_s, *, tm, ce):
    e = pl.program_id(1)
    nk = PEER_NKEYS
    h2 = h2_ref[...]

    @pl.when(e == 0)
    def _select():
        qt = lax.dot_general(wqt_ref[...], h2, (((1,), (1,)), ((), ())),
                             preferred_element_type=F32)
        tops = []
        for hc in range(2 * PEER_HEADS):
            q = qt[hc * PEER_KEY_DIM:(hc + 1) * PEER_KEY_DIM].astype(BF16)
            s = jnp.dot(keys_ref[hc], q, preferred_element_type=F32)
            s_s[hc] = s
            tops.append(_top_values(s, PEER_TOPK))
        for h in range(PEER_HEADS):
            v1, v2 = tops[2 * h], tops[2 * h + 1]
            cand = jnp.concatenate([v1[a:a + 1] + v2 for a in range(PEER_TOPK)], axis=0)
            tau = _top_values(cand, PEER_TOPK)[PEER_TOPK - 1:PEER_TOPK]
            top = v1[0:1] + v2[0:1]
            z = jnp.sum(jnp.where(cand >= tau, jnp.exp(cand - top), 0.0), axis=0, keepdims=True)
            s1, s2 = s_s[2 * h], s_s[2 * h + 1]
            e_s[2 * h] = jnp.where(s1 >= v1[PEER_TOPK - 1:], jnp.exp(s1 - v1[0:1]), 0.0) / z
            e_s[2 * h + 1] = jnp.where(s2 >= v2[PEER_TOPK - 1:], jnp.exp(s2 - v2[0:1]), 0.0)
            tau_s[h] = jnp.broadcast_to(tau, (8, tm))
        acc_s[...] = jnp.zeros(acc_s.shape, F32)

    act = lax.dot_general(u_ref[...], h2, (((1,), (1,)), ((), ())),
                          preferred_element_type=F32)
    ps = []
    for ii in range(ce // nk):
        i = e * (ce // nk) + ii
        g = jnp.zeros((nk, tm), F32)
        for h in range(PEER_HEADS):
            s1row = s_s[2 * h, pl.ds(i, 1), :]
            e1row = e_s[2 * h, pl.ds(i, 1), :]
            keep = (s1row + s_s[2 * h + 1]) >= tau_s[h, 0:1, :]
            g = g + e1row * jnp.where(keep, e_s[2 * h + 1], 0.0)
        ps.append((g * _gelu_exact(act[ii * nk:(ii + 1) * nk])).astype(BF16))
    acc_s[...] += jnp.dot(vt_ref[...], jnp.concatenate(ps, axis=0), preferred_element_type=F32)

    @pl.when(e == pl.num_programs(1) - 1)
    def _fin():
        y = x1_ref[...] + g2_ref[0] * acc_s[...].T
        ms = jnp.mean(y * y, axis=-1, keepdims=True)
        o_ref[...] = y * lax.rsqrt(ms + NORM_EPS) * fg_ref[...]


def _peer(h2, x1, g2, wqt, keys, u, vt, final_g, S, tm=256, ce=512):
    N, D = h2.shape
    n_exp = u.shape[0]
    B = g2.shape[0]
    kern = functools.partial(_peer_kernel, tm=tm, ce=ce)
    return pl.pallas_call(
        kern,
        grid=(N // tm, n_exp // ce),
        in_specs=[pl.BlockSpec((tm, D), lambda t, e: (t, 0)),
                  pl.BlockSpec((tm, D), lambda t, e: (t, 0)),
                  pl.BlockSpec((1, 1, D), lambda t, e: ((t * tm) // S, 0, 0)),
                  pl.BlockSpec(wqt.shape, lambda t, e: (0, 0)),
                  pl.BlockSpec(keys.shape, lambda t, e: (0, 0, 0)),
                  pl.BlockSpec((ce, D), lambda t, e: (e, 0)),
                  pl.BlockSpec((D, ce), lambda t, e: (0, e)),
                  pl.BlockSpec((1, D), lambda t, e: (0, 0))],
        out_specs=pl.BlockSpec((tm, D), lambda t, e: (t, 0)),
        out_shape=jax.ShapeDtypeStruct((N, D), F32),
        scratch_shapes=[pltpu.VMEM((2 * PEER_HEADS, PEER_NKEYS, tm), F32),
                        pltpu.VMEM((2 * PEER_HEADS, PEER_NKEYS, tm), F32),
                        pltpu.VMEM((PEER_HEADS, 8, tm), F32),
                        pltpu.VMEM((D, tm), F32)],
        compiler_params=_params(("parallel", "arbitrary")),
        name="peer",
    )(h2, x1, g2.reshape(B, 1, D), wqt, keys, u, vt, final_g.reshape(1, D))


def kernel(x, c, rel_bias, w_ada, b_ada, norm1_g, w_in, diff_lambda, diff_subln_g, w_out, norm2_g,
           peer_wq, peer_keys, peer_u, peer_v, final_g):
    B, S, D = x.shape
    assert w_ada.shape[0] == 1, "single-layer kernel"
    assert S % MOBA_BLOCK == 0 and D == D_MODEL
    diff_tiles, moba_tiles = _bias_tiles(rel_bias, S, MOBA_BLOCK)

    mod = _modulation(c, w_ada[0], b_ada[0])
    sh1, sc1, g1, sh2, sc2, g2 = jnp.split(mod, 6, axis=-1)

    proj, kmean = _in_proj(x, norm1_g[0], sc1, sh1, w_in[0].astype(BF16))
    nblk = S // MOBA_BLOCK
    kmt = jnp.pad(kmean.reshape(B, nblk, 512).transpose(0, 2, 1), ((0, 0), (0, 0), (0, HEAD_COLS - nblk)))

    d_out = _attention(proj, diff_tiles, (diff_lambda[0], diff_subln_g[0]), moba=False, col0=0)
    m_out = _attention(proj, moba_tiles, (kmt,), moba=True, col0=12)

    x1, h2 = _out_proj(d_out, m_out, w_out[0].astype(BF16), x, g1, norm2_g[0], sc2, sh2)

    wqt = peer_wq[0].T.astype(BF16)
    keys = peer_keys[0].reshape(2 * PEER_HEADS, PEER_NKEYS, PEER_KEY_DIM).astype(BF16)
    out = _peer(h2.reshape(B * S, D), x1.reshape(B * S, D), g2, wqt, keys,
                peer_u[0].astype(BF16), peer_v[0].T.astype(BF16), final_g, S)
    return out.reshape(B, S, D)
```

```python
import functools
import math

import jax
import jax.numpy as jnp
from jax import lax
from jax.experimental import pallas as pl
from jax.experimental.pallas import tpu as pltpu

F32 = jnp.float32
BF16 = jnp.bfloat16

D_MODEL = 1024
DIFF_HEADS = 4
MOBA_HEADS = 8
HEAD_COLS = 128
HALF = 64
MOBA_BLOCK = 256
MOBA_TOPK = 3
IN_COLS = 3072
REL_BUCKETS = 32
REL_MAX_DIST = 128
PEER_HEADS = 8
PEER_NKEYS = 128
PEER_KEY_DIM = 128
PEER_TOPK = 16
NORM_EPS = 1e-6
NEG_INF = -1e30
LAMBDA_INIT = 0.8 - 0.6 * math.exp(-0.3 * 0)

VMEM_LIMIT_BYTES = 56 * 1024 * 1024


def _params(semantics):
    return pltpu.CompilerParams(dimension_semantics=semantics, vmem_limit_bytes=VMEM_LIMIT_BYTES)


def _mod_kernel(c_ref, w_ref, b_ref, o_ref):
    c = c_ref[...]
    act = c * jax.nn.sigmoid(c)
    o_ref[...] = jnp.dot(act, w_ref[...], preferred_element_type=F32,
                         precision=lax.Precision.HIGHEST) + b_ref[...]


def _modulation(c, w_ada, b_ada):
    B, D = c.shape
    n_out = w_ada.shape[1]
    tn = 1024
    return pl.pallas_call(
        _mod_kernel,
        grid=(n_out // tn,),
        in_specs=[pl.BlockSpec((B, D), lambda n: (0, 0)),
                  pl.BlockSpec((D, tn), lambda n: (0, n)),
                  pl.BlockSpec((1, tn), lambda n: (0, n))],
        out_specs=pl.BlockSpec((B, tn), lambda n: (0, n)),
        out_shape=jax.ShapeDtypeStruct((B, n_out), F32),
        compiler_params=_params(("arbitrary",)),
        name="mod",
    )(c, w_ada, b_ada.reshape(1, n_out))


def _rms_modulate(x, g, sc, sh):
    ms = jnp.mean(x * x, axis=-1, keepdims=True)
    return (x * lax.rsqrt(ms + NORM_EPS) * g) * (1.0 + sc) + sh


def _in_kernel(x_ref, g_ref, sc_ref, sh_ref, w_ref, proj_ref, kmean_ref, *, tm, cn, mk_chunk):
    h = _rms_modulate(x_ref[0], g_ref[...], sc_ref[0], sh_ref[0]).astype(BF16)
    for n in range(IN_COLS // cn):
        acc = jnp.dot(h, w_ref[:, n * cn:(n + 1) * cn], preferred_element_type=F32)
        proj_ref[0, :, n * cn:(n + 1) * cn] = acc.astype(BF16)
        if n == mk_chunk:
            for r in range(tm // MOBA_BLOCK):
                kmean_ref[0, r] = jnp.mean(acc[r * MOBA_BLOCK:(r + 1) * MOBA_BLOCK], axis=0, keepdims=True)


def _in_proj(x, g, sc, sh, w_in_bf16, tm=512):
    B, S, D = x.shape
    cn = 512
    mk_chunk = 4
    nblk = S // MOBA_BLOCK
    kern = functools.partial(_in_kernel, tm=tm, cn=cn, mk_chunk=mk_chunk)
    vec = pl.BlockSpec((1, 1, D), lambda b, i: (b, 0, 0))
    return pl.pallas_call(
        kern,
        grid=(B, S // tm),
        in_specs=[pl.BlockSpec((1, tm, D), lambda b, i: (b, i, 0)),
                  pl.BlockSpec((1, D), lambda b, i: (0, 0)),
                  vec, vec,
                  pl.BlockSpec((D, IN_COLS), lambda b, i: (0, 0))],
        out_specs=[pl.BlockSpec((1, tm, IN_COLS), lambda b, i: (b, i, 0)),
                   pl.BlockSpec((1, tm // MOBA_BLOCK, 1, 512), lambda b, i: (b, i, 0, 0))],
        out_shape=[jax.ShapeDtypeStruct((B, S, IN_COLS), BF16),
                   jax.ShapeDtypeStruct((B, nblk, 1, 512), F32)],
        compiler_params=_params(("parallel", "arbitrary")),
        name="in_proj",
    )(x, g.reshape(1, D), sc.reshape(B, 1, D), sh.reshape(B, 1, D), w_in_bf16)


def _attn_kernel(*refs, moba, tq, tk, nblk):
    if moba:
        q_ref, k_ref, v_ref, bias_ref, km_ref, o_ref, q2_s, s_s, p_s, m_s, l_s, acc_s = refs
    else:
        q_ref, k_ref, v_ref, bias_ref, lam_ref, g_ref, o_ref, q2_s, s_s, p_s, m_s, l_s, acc_s = refs
    i = pl.program_id(2)
    lane = lax.broadcasted_iota(jnp.int32, (tq, HEAD_COLS), 1)
    reps = tk // HEAD_COLS
    nt = (((1,), (1,)), ((), ()))

    qs = q_ref[0] * 0.125
    q_lo = jnp.where(lane < HALF, qs, jnp.zeros_like(qs))
    q_hi = jnp.where(lane >= HALF, qs, jnp.zeros_like(qs))
    q2_s[0:tq, 0:HEAD_COLS] = q_lo
    q2_s[tq:2 * tq, 0:HEAD_COLS] = q_hi
    if moba:
        q2 = jnp.concatenate([q_lo, q_hi], axis=0).astype(F32)
        gate = lax.dot_general(km_ref[0], q2, nt, preferred_element_type=F32,
                               precision=lax.Precision.HIGHEST)
        nb8 = -(-nblk // 8) * 8
        gate = gate[0:nb8]
        blk = lax.broadcasted_iota(jnp.int32, (nb8, 2 * tq), 0)
        cnt = jnp.zeros((nb8, 2 * tq), jnp.int32)
        for jp in range(nblk):
            row = gate[jp:jp + 1]
            beats = (row > gate) | ((row == gate) & (blk > jp))
            cnt = cnt + jnp.where(beats, jnp.where(i > jp, 1, 0), 0)
        sel = ((blk < i) & (cnt < MOBA_TOPK)) | (blk == i)
        sb = jnp.where(sel, 0.0, NEG_INF).astype(F32)
        sb = jnp.concatenate([sb, jnp.zeros((HEAD_COLS - nb8, 2 * tq), F32)], axis=0)
        q2_s[:, HEAD_COLS:2 * HEAD_COLS] = sb.T.astype(BF16)

    def tile_rows(j):
        return pl.ds(pl.multiple_of(j * tk, tk), tk)

    def qk(j):
        kt = k_ref[0, tile_rows(j), :]
        if moba:
            lane_k = lax.broadcasted_iota(jnp.int32, (tk, HEAD_COLS), 1)
            kt = jnp.concatenate([kt, jnp.where(lane_k == j, 1.0, 0.0).astype(BF16)], axis=1)
        return lax.dot_general(q2_s[...], kt, nt, preferred_element_type=F32)

    def stage(j, kind, with_qk):
        cur = j % 2
        nxt = 1 - cur
        pv = jnp.dot(p_s[nxt], v_ref[0, tile_rows(jnp.maximum(j - 1, 0)), :], preferred_element_type=F32)
        s = s_s[cur]
        for m in range(2):
            half = slice(m * tq, (m + 1) * tq)
            sm = s[half]
            if kind is not None:
                sm = sm + bias_ref[0, m, kind]
            m_prev = m_s[m]
            m_new = jnp.maximum(m_prev, jnp.max(sm, axis=1, keepdims=True))
            alpha = jnp.exp(m_prev - m_new)
            p = jnp.exp(sm - jnp.concatenate([m_new] * reps, axis=1))
            psum = p[:, 0:HEAD_COLS]
            for r in range(1, reps):
                psum = psum + p[:, r * HEAD_COLS:(r + 1) * HEAD_COLS]
            l_s[m] = alpha * l_s[m] + psum
            m_s[m] = m_new
            acc_s[m] = alpha * (acc_s[m] + pv[half])
            p_s[cur, half] = p.astype(BF16)
        if with_qk:
            s_s[nxt] = qk(j + 1)

    m_s[...] = jnp.full(m_s.shape, -jnp.inf, F32)
    l_s[...] = jnp.zeros(l_s.shape, F32)
    acc_s[...] = jnp.zeros(acc_s.shape, F32)
    p_s[1] = jnp.zeros(p_s.shape[1:], BF16)
    s_s[0] = qk(0)

    def far(j, carry):
        stage(j, None, True)
        return carry

    lax.fori_loop(0, jnp.maximum(i - 1, 0), far, 0)

    @pl.when(i >= 1)
    def _near():
        stage(i - 1, 1, True)

    stage(i, 0, False)
    pv = jnp.dot(p_s[i % 2], v_ref[0, tile_rows(i), :], preferred_element_type=F32)
    o0 = (acc_s[0] + pv[0:tq]) / jnp.sum(l_s[0], axis=1, keepdims=True)
    o1 = (acc_s[1] + pv[tq:2 * tq]) / jnp.sum(l_s[1], axis=1, keepdims=True)
    if moba:
        o = jnp.where(lane < HALF, o0, o1)
    else:
        lp = lam_ref[...]
        lam = (jnp.exp(jnp.sum(lp[0:1] * lp[1:2], axis=1, keepdims=True))
               - jnp.exp(jnp.sum(lp[2:3] * lp[3:4], axis=1, keepdims=True)) + LAMBDA_INIT)
        d = o0 - lam * o1
        ms = jnp.mean(d * d, axis=-1, keepdims=True)
        o = (d * lax.rsqrt(ms + NORM_EPS) * g_ref[...]) * (1.0 - LAMBDA_INIT)
    o_ref[0] = o.astype(BF16)


def _attention(proj, bias_tiles, extra, *, moba, col0):
    B, S, _ = proj.shape
    tq = tk = MOBA_BLOCK
    nq = S // tq
    nblk = S // MOBA_BLOCK
    kern = functools.partial(_attn_kernel, moba=moba, tq=tq, tk=tk, nblk=nblk)
    in_specs = [
        pl.BlockSpec((1, tq, HEAD_COLS), lambda b, h, i: (b, i, col0 + h)),
        pl.BlockSpec((1, S, HEAD_COLS), lambda b, h, i: (b, 0, col0 + 4 + h)),
        pl.BlockSpec((1, S, HEAD_COLS), lambda b, h, i: (b, 0, col0 + 8 + h)),
        pl.BlockSpec((1, 2, 2, tq, tk), lambda b, h, i: (h, 0, 0, 0, 0)),
    ]
    if moba:
        (km,) = extra
        in_specs.append(pl.BlockSpec((1, HEAD_COLS, HEAD_COLS), lambda b, h, i: (b, 0, h)))
        args = (proj, proj, proj, bias_tiles, km)
    else:
        lam_p, subln_g = extra
        in_specs.append(pl.BlockSpec(lam_p.shape, lambda b, h, i: (0, 0)))
        in_specs.append(pl.BlockSpec((1, HEAD_COLS), lambda b, h, i: (0, 0)))
        args = (proj, proj, proj, bias_tiles, lam_p, subln_g.reshape(1, HEAD_COLS))
    scratch = [pltpu.VMEM((2 * tq, 2 * HEAD_COLS if moba else HEAD_COLS), BF16),
               pltpu.VMEM((2, 2 * tq, tk), F32),
               pltpu.VMEM((2, 2 * tq, tk), BF16),
               pltpu.VMEM((2, tq, HEAD_COLS), F32),
               pltpu.VMEM((2, tq, HEAD_COLS), F32),
               pltpu.VMEM((2, tq, HEAD_COLS), F32)]
    return pl.pallas_call(
        kern,
        grid=(B, 4, nq),
        in_specs=in_specs,
        out_specs=pl.BlockSpec((1, tq, HEAD_COLS), lambda b, h, i: (b, i, h)),
        out_shape=jax.ShapeDtypeStruct((B, S, 4 * HEAD_COLS), BF16),
        scratch_shapes=scratch,
        compiler_params=_params(("parallel", "parallel", "arbitrary")),
        name="moba_attn" if moba else "diff_attn",
    )(*args)


def _pad_kmean(kmean):
    return jnp.pad(kmean, ((0, 0), (0, HEAD_COLS - kmean.shape[1]), (0, 0)))


def _t5_bucket(n):
    max_exact = REL_BUCKETS // 2
    nf = jnp.maximum(n, 1).astype(F32)
    large = max_exact + (jnp.log(nf / max_exact) / math.log(REL_MAX_DIST / max_exact)
                         * (REL_BUCKETS - max_exact)).astype(jnp.int32)
    large = jnp.minimum(large, REL_BUCKETS - 1)
    return jnp.where(n < max_exact, n, large)


def _bias_tiles(rel_bias, S, t):
    assert t >= REL_MAX_DIST
    by_dist = rel_bias[_t5_bucket(jnp.arange(S))]
    far = rel_bias[REL_BUCKETS - 1]
    r = jnp.arange(t)[:, None]
    c = jnp.arange(t)[None, :]
    tiles = []
    for kind in range(2):
        dist = kind * t + r - c
        tile = by_dist[jnp.clip(dist, 0, S - 1)] - far
        if kind == 0:
            tile = jnp.where((dist >= 0)[..., None], tile, NEG_INF)
        tiles.append(tile)
    tiles = jnp.stack(tiles, axis=0).transpose(3, 0, 1, 2)
    diff = tiles[:2 * DIFF_HEADS].reshape(2, DIFF_HEADS, 2, t, t).transpose(1, 0, 2, 3, 4)
    moba = tiles[2 * DIFF_HEADS:].reshape(MOBA_HEADS // 2, 2, 2, t, t)
    return diff, moba


def _out_kernel(d_ref, m_ref, wd_ref, wm_ref, x_ref, g1_ref, ng_ref, sc_ref, sh_ref, x1_ref, h2_ref):
    mix = (jnp.dot(d_ref[0], wd_ref[...], preferred_element_type=F32)
           + jnp.dot(m_ref[0], wm_ref[...], preferred_element_type=F32))
    x1 = x_ref[0] + g1_ref[0] * mix
    x1_ref[0] = x1
    h2_ref[0] = _rms_modulate(x1, ng_ref[...], sc_ref[0], sh_ref[0]).astype(BF16)


def _out_proj(d_out, m_out, w_out_bf16, x, g1, norm_g, sc, sh, tm=512):
    B, S, D = x.shape
    half = d_out.shape[-1]
    vec = pl.BlockSpec((1, 1, D), lambda b, i: (b, 0, 0))
    tok = lambda w: pl.BlockSpec((1, tm, w), lambda b, i: (b, i, 0))
    return pl.pallas_call(
        _out_kernel,
        grid=(B, S // tm),
        in_specs=[tok(half), tok(half),
                  pl.BlockSpec((half, D), lambda b, i: (0, 0)),
                  pl.BlockSpec((half, D), lambda b, i: (1, 0)),
                  tok(D), vec,
                  pl.BlockSpec((1, D), lambda b, i: (0, 0)),
                  vec, vec],
        out_specs=[tok(D), tok(D)],
        out_shape=[jax.ShapeDtypeStruct((B, S, D), F32), jax.ShapeDtypeStruct((B, S, D), BF16)],
        compiler_params=_params(("parallel", "arbitrary")),
        name="out_proj",
    )(d_out, m_out, w_out_bf16, w_out_bf16, x, g1.reshape(B, 1, D), norm_g.reshape(1, D),
      sc.reshape(B, 1, D), sh.reshape(B, 1, D))


def _top_values(w, k):
    row = lax.broadcasted_iota(jnp.int32, (k, w.shape[1]), 0)
    out = jnp.zeros((k, w.shape[1]), F32)
    for r in range(k):
        m = jnp.max(w, axis=0, keepdims=True)
        out = jnp.where(row == r, m, out)
        w = jnp.where(w == m, -jnp.inf, w)
    return out


def _gelu_exact(a):
    return 0.5 * a * (1.0 + lax.erf(a * math.sqrt(0.5)))


def _peer_kernel(h2_ref, x1_ref, g2_ref, wqt_ref, keys_ref, u_ref, vt_ref, fg_ref, o_ref,
                 h2t_s, s2_s, e1_s, e2_s, th_s, act_s, p_s, acc_s, *, tm, ce, n_chunks):
    s = pl.program_id(1)
    nk = PEER_NKEYS

    @pl.when(s == 0)
    def _select():
        h2t_s[...] = h2_ref[...].astype(F32).T.astype(BF16)
        qt = jnp.dot(wqt_ref[...], h2t_s[...], preferred_element_type=F32)
        inf = jnp.full((1, tm), jnp.inf, F32)
        for h in range(PEER_HEADS):
            sc, top = [], []
            for c in range(2):
                hc = 2 * h + c
                q = qt[hc * PEER_KEY_DIM:(hc + 1) * PEER_KEY_DIM].astype(BF16)
                sc.append(jnp.dot(keys_ref[hc], q, preferred_element_type=F32))
                top.append(_top_values(sc[c], PEER_TOPK))
            (s1, s2), (v1, v2) = sc, top
            cand = [v1[a:a + 1] + v2 for a in range(PEER_TOPK)]
            tau = _top_values(jnp.concatenate(cand, axis=0), PEER_TOPK)[PEER_TOPK - 1:PEER_TOPK]
            peak = v1[0:1] + v2[0:1]
            z = jnp.zeros((1, tm), F32)
            th = jnp.broadcast_to(inf, (nk, tm))
            for a in range(PEER_TOPK):
                keep = cand[a] >= tau
                z = z + jnp.sum(jnp.where(keep, jnp.exp(cand[a] - peak), 0.0), axis=0, keepdims=True)
                th_a = jnp.min(jnp.where(keep, v2, jnp.inf), axis=0, keepdims=True)
                th = jnp.where(s1 == v1[a:a + 1], th_a, th)
            th_s[h] = th
            s2_s[h] = s2
            e1_s[h] = jnp.where(s1 >= v1[PEER_TOPK - 1:], jnp.exp(s1 - v1[0:1]), 0.0) / z
            e2_s[h] = jnp.where(s2 >= v2[PEER_TOPK - 1:], jnp.exp(s2 - v2[0:1]), 0.0)
        act_s[...] = jnp.zeros(act_s.shape, F32)
        p_s[...] = jnp.zeros(p_s.shape, BF16)
        acc_s[...] = jnp.zeros(acc_s.shape, F32)

    cur = s % 2
    prev = 1 - cur
    acc_s[...] += jnp.dot(vt_ref[...], p_s[cur], preferred_element_type=F32)

    chunk = jnp.clip(s - 1, 0, n_chunks - 1)
    for ii in range(ce // nk):
        i = chunk * (ce // nk) + ii
        g = jnp.zeros((nk, tm), F32)
        for h in range(PEER_HEADS):
            th_row = th_s[h, pl.ds(i, 1), :]
            e1_row = e1_s[h, pl.ds(i, 1), :]
            g = g + e1_row * jnp.where(s2_s[h] >= th_row, e2_s[h], 0.0)
        rows = slice(ii * nk, (ii + 1) * nk)
        p_s[prev, rows] = (g * _gelu_exact(act_s[prev, rows])).astype(BF16)

    act_s[cur] = jnp.dot(u_ref[...], h2t_s[...], preferred_element_type=F32)

    @pl.when(s == pl.num_programs(1) - 1)
    def _fin():
        y = x1_ref[...] + g2_ref[0] * acc_s[...].T
        ms = jnp.mean(y * y, axis=-1, keepdims=True)
        o_ref[...] = y * lax.rsqrt(ms + NORM_EPS) * fg_ref[...]


def _peer(h2, x1, g2, wqt, keys, u, vt, final_g, S, tm=256, ce=512):
    N, D = h2.shape
    n_chunks = u.shape[0] // ce
    B = g2.shape[0]
    kern = functools.partial(_peer_kernel, tm=tm, ce=ce, n_chunks=n_chunks)
    head_rows = pltpu.VMEM((PEER_HEADS, PEER_NKEYS, tm), F32)
    return pl.pallas_call(
        kern,
        grid=(N // tm, n_chunks + 2),
        in_specs=[pl.BlockSpec((tm, D), lambda t, s: (t, 0)),
                  pl.BlockSpec((tm, D), lambda t, s: (t, 0)),
                  pl.BlockSpec((1, 1, D), lambda t, s: ((t * tm) // S, 0, 0)),
                  pl.BlockSpec(wqt.shape, lambda t, s: (0, 0)),
                  pl.BlockSpec(keys.shape, lambda t, s: (0, 0, 0)),
                  pl.BlockSpec((ce, D), lambda t, s: (jnp.minimum(s, n_chunks - 1), 0)),
                  pl.BlockSpec((D, ce), lambda t, s: (0, jnp.clip(s - 2, 0, n_chunks - 1))),
                  pl.BlockSpec((1, D), lambda t, s: (0, 0))],
        out_specs=pl.BlockSpec((tm, D), lambda t, s: (t, 0)),
        out_shape=jax.ShapeDtypeStruct((N, D), F32),
        scratch_shapes=[pltpu.VMEM((D, tm), BF16),
                        head_rows, head_rows, head_rows, head_rows,
                        pltpu.VMEM((2, ce, tm), F32),
                        pltpu.VMEM((2, ce, tm), BF16),
                        pltpu.VMEM((D, tm), F32)],
        compiler_params=_params(("parallel", "arbitrary")),
        name="peer",
    )(h2, x1, g2.reshape(B, 1, D), wqt, keys, u, vt, final_g.reshape(1, D))


def kernel(x, c, rel_bias, w_ada, b_ada, norm1_g, w_in, diff_lambda, diff_subln_g, w_out, norm2_g,
           peer_wq, peer_keys, peer_u, peer_v, final_g):
    B, S, D = x.shape
    assert w_ada.shape[0] == 1, "single-layer kernel"
    assert S % MOBA_BLOCK == 0 and D == D_MODEL
    diff_tiles, moba_tiles = _bias_tiles(rel_bias, S, MOBA_BLOCK)

    mod = _modulation(c, w_ada[0], b_ada[0])
    sh1, sc1, g1, sh2, sc2, g2 = jnp.split(mod, 6, axis=-1)

    proj, kmean = _in_proj(x, norm1_g[0], sc1, sh1, w_in[0].astype(BF16))
    km = _pad_kmean(kmean.reshape(B, S // MOBA_BLOCK, 512))

    d_out = _attention(proj, diff_tiles, (diff_lambda[0], diff_subln_g[0]), moba=False, col0=0)
    m_out = _attention(proj, moba_tiles, (km,), moba=True, col0=12)

    x1, h2 = _out_proj(d_out, m_out, w_out[0].astype(BF16), x, g1, norm2_g[0], sc2, sh2)

    wqt = peer_wq[0].T.astype(BF16)
    keys = peer_keys[0].reshape(2 * PEER_HEADS, PEER_NKEYS, PEER_KEY_DIM).astype(BF16)
    out = _peer(h2.reshape(B * S, D), x1.reshape(B * S, D), g2, wqt, keys,
                peer_u[0].astype(BF16), peer_v[0].T.astype(BF16), final_g, S)
    return out.reshape(B, S, D)
```

```python
import functools
import math

import jax
import jax.numpy as jnp
from jax import lax
from jax.experimental import pallas as pl
from jax.experimental.pallas import tpu as pltpu

F32 = jnp.float32
BF16 = jnp.bfloat16

D_MODEL = 1024
DIFF_HEADS = 4
MOBA_HEADS = 8
HEAD_COLS = 128
HALF = 64
MOBA_BLOCK = 256
MOBA_TOPK = 3
QV_COLS = 2048
K_COLS = 1024
REL_BUCKETS = 32
REL_MAX_DIST = 128
PEER_HEADS = 8
PEER_NKEYS = 128
PEER_KEY_DIM = 128
PEER_TOPK = 16
NORM_EPS = 1e-6
NEG_INF = -1e30
LAMBDA_INIT = 0.8 - 0.6 * math.exp(-0.3 * 0)

VMEM_LIMIT_BYTES = 56 * 1024 * 1024


def _params(semantics):
    return pltpu.CompilerParams(dimension_semantics=semantics, vmem_limit_bytes=VMEM_LIMIT_BYTES)


def _mod_kernel(c_ref, w_ref, b_ref, o_ref):
    c = c_ref[...]
    act = c * jax.nn.sigmoid(c)
    o_ref[...] = jnp.dot(act, w_ref[...], preferred_element_type=F32,
                         precision=lax.Precision.HIGHEST) + b_ref[...]


def _modulation(c, w_ada, b_ada):
    B, D = c.shape
    n_out = w_ada.shape[1]
    tn = 1024
    return pl.pallas_call(
        _mod_kernel,
        grid=(n_out // tn,),
        in_specs=[pl.BlockSpec((B, D), lambda n: (0, 0)),
                  pl.BlockSpec((D, tn), lambda n: (0, n)),
                  pl.BlockSpec((1, tn), lambda n: (0, n))],
        out_specs=pl.BlockSpec((B, tn), lambda n: (0, n)),
        out_shape=jax.ShapeDtypeStruct((B, n_out), F32),
        compiler_params=_params(("arbitrary",)),
        name="mod",
    )(c, w_ada, b_ada.reshape(1, n_out))


def _rms_modulate(x, g, sc, sh):
    ms = jnp.mean(x * x, axis=-1, keepdims=True)
    return (x * lax.rsqrt(ms + NORM_EPS) * g) * (1.0 + sc) + sh


def _in_kernel(x_ref, g_ref, sc_ref, sh_ref, wqv_ref, wk_ref, qv_ref, kt_ref, kmean_ref, *, tm, cn):
    h = _rms_modulate(x_ref[0], g_ref[...], sc_ref[0], sh_ref[0]).astype(BF16)
    for n in range(QV_COLS // cn):
        cols = slice(n * cn, (n + 1) * cn)
        qv_ref[0, :, cols] = jnp.dot(h, wqv_ref[:, cols], preferred_element_type=F32).astype(BF16)
    for n in range(K_COLS // cn):
        cols = slice(n * cn, (n + 1) * cn)
        k = jnp.dot(h, wk_ref[:, cols], preferred_element_type=F32)
        kt_ref[0, cols, :] = k.T.astype(BF16)
        if n == 1:
            for r in range(tm // MOBA_BLOCK):
                kmean_ref[0, r] = jnp.mean(k[r * MOBA_BLOCK:(r + 1) * MOBA_BLOCK], axis=0, keepdims=True)


def _in_proj(x, g, sc, sh, w_qv, w_k, tm=512):
    B, S, D = x.shape
    cn = 512
    nblk = S // MOBA_BLOCK
    kern = functools.partial(_in_kernel, tm=tm, cn=cn)
    vec = pl.BlockSpec((1, 1, D), lambda b, i: (b, 0, 0))
    return pl.pallas_call(
        kern,
        grid=(B, S // tm),
        in_specs=[pl.BlockSpec((1, tm, D), lambda b, i: (b, i, 0)),
                  pl.BlockSpec((1, D), lambda b, i: (0, 0)),
                  vec, vec,
                  pl.BlockSpec((D, QV_COLS), lambda b, i: (0, 0)),
                  pl.BlockSpec((D, K_COLS), lambda b, i: (0, 0))],
        out_specs=[pl.BlockSpec((1, tm, QV_COLS), lambda b, i: (b, i, 0)),
                   pl.BlockSpec((1, K_COLS, tm), lambda b, i: (b, 0, i)),
                   pl.BlockSpec((1, tm // MOBA_BLOCK, 1, 512), lambda b, i: (b, i, 0, 0))],
        out_shape=[jax.ShapeDtypeStruct((B, S, QV_COLS), BF16),
                   jax.ShapeDtypeStruct((B, K_COLS, S), BF16),
                   jax.ShapeDtypeStruct((B, nblk, 1, 512), F32)],
        compiler_params=_params(("parallel", "arbitrary")),
        name="in_proj",
    )(x, g.reshape(1, D), sc.reshape(B, 1, D), sh.reshape(B, 1, D), w_qv, w_k)


def _attn_kernel(*refs, moba, tq, tk, nblk):
    if moba:
        q_ref, kt_ref, v_ref, bias_ref, km_ref, o_ref, q2_s, s_s, p_s, m_s, l_s, acc_s = refs
    else:
        q_ref, kt_ref, v_ref, bias_ref, lam_ref, g_ref, o_ref, q2_s, s_s, p_s, m_s, l_s, acc_s = refs
    i = pl.program_id(2)
    lane = lax.broadcasted_iota(jnp.int32, (tq, HEAD_COLS), 1)
    reps = tk // HEAD_COLS

    qs = q_ref[0] * 0.125
    q_lo = jnp.where(lane < HALF, qs, jnp.zeros_like(qs))
    q_hi = jnp.where(lane >= HALF, qs, jnp.zeros_like(qs))
    q2_s[0:tq, 0:HEAD_COLS] = q_lo
    q2_s[tq:2 * tq, 0:HEAD_COLS] = q_hi
    if moba:
        q2 = jnp.concatenate([q_lo, q_hi], axis=0).astype(F32)
        gate = lax.dot_general(km_ref[0], q2, (((1,), (1,)), ((), ())), preferred_element_type=F32,
                               precision=lax.Precision.HIGHEST)
        nb8 = -(-nblk // 8) * 8
        gate = gate[0:nb8]
        blk = lax.broadcasted_iota(jnp.int32, (nb8, 2 * tq), 0)
        cnt = jnp.zeros((nb8, 2 * tq), jnp.int32)
        for jp in range(nblk):
            row = gate[jp:jp + 1]
            beats = (row > gate) | ((row == gate) & (blk > jp))
            cnt = cnt + jnp.where(beats, jnp.where(i > jp, 1, 0), 0)
        sel = ((blk < i) & (cnt < MOBA_TOPK)) | (blk == i)
        sb = jnp.where(sel, 0.0, NEG_INF).astype(F32)
        sb = jnp.concatenate([sb, jnp.zeros((HEAD_COLS - nb8, 2 * tq), F32)], axis=0)
        q2_s[:, HEAD_COLS:2 * HEAD_COLS] = sb.T.astype(BF16)

    def tile_rows(j):
        return pl.ds(pl.multiple_of(j * tk, tk), tk)

    def qk(j):
        kt = kt_ref[0, :, tile_rows(j)]
        if moba:
            row_k = lax.broadcasted_iota(jnp.int32, (HEAD_COLS, tk), 0)
            kt = jnp.concatenate([kt, jnp.where(row_k == j, 1.0, 0.0).astype(BF16)], axis=0)
        return jnp.dot(q2_s[...], kt, preferred_element_type=F32)

    def stage(j, kind, with_qk):
        cur = j % 2
        nxt = 1 - cur
        pv = jnp.dot(p_s[nxt], v_ref[0, tile_rows(jnp.maximum(j - 1, 0)), :], preferred_element_type=F32)
        s = s_s[cur]
        for m in range(2):
            half = slice(m * tq, (m + 1) * tq)
            sm = s[half]
            if kind is not None:
                sm = sm + bias_ref[0, m, kind]
            m_prev = m_s[m]
            m_new = jnp.maximum(m_prev, jnp.max(sm, axis=1, keepdims=True))
            alpha = jnp.exp(m_prev - m_new)
            p = jnp.exp(sm - jnp.concatenate([m_new] * reps, axis=1))
            psum = p[:, 0:HEAD_COLS]
            for r in range(1, reps):
                psum = psum + p[:, r * HEAD_COLS:(r + 1) * HEAD_COLS]
            l_s[m] = alpha * l_s[m] + psum
            m_s[m] = m_new
            acc_s[m] = alpha * (acc_s[m] + pv[half])
            p_s[cur, half] = p.astype(BF16)
        if with_qk:
            s_s[nxt] = qk(j + 1)

    m_s[...] = jnp.full(m_s.shape, -jnp.inf, F32)
    l_s[...] = jnp.zeros(l_s.shape, F32)
    acc_s[...] = jnp.zeros(acc_s.shape, F32)
    p_s[1] = jnp.zeros(p_s.shape[1:], BF16)
    s_s[0] = qk(0)

    def far(j, carry):
        stage(j, None, True)
        return carry

    lax.fori_loop(0, jnp.maximum(i - 1, 0), far, 0)

    @pl.when(i >= 1)
    def _near():
        stage(i - 1, 1, True)

    stage(i, 0, False)
    pv = jnp.dot(p_s[i % 2], v_ref[0, tile_rows(i), :], preferred_element_type=F32)
    o0 = (acc_s[0] + pv[0:tq]) / jnp.sum(l_s[0], axis=1, keepdims=True)
    o1 = (acc_s[1] + pv[tq:2 * tq]) / jnp.sum(l_s[1], axis=1, keepdims=True)
    if moba:
        o = jnp.where(lane < HALF, o0, o1)
    else:
        lp = lam_ref[...]
        lam = (jnp.exp(jnp.sum(lp[0:1] * lp[1:2], axis=1, keepdims=True))
               - jnp.exp(jnp.sum(lp[2:3] * lp[3:4], axis=1, keepdims=True)) + LAMBDA_INIT)
        d = o0 - lam * o1
        ms = jnp.mean(d * d, axis=-1, keepdims=True)
        o = (d * lax.rsqrt(ms + NORM_EPS) * g_ref[...]) * (1.0 - LAMBDA_INIT)
    o_ref[0] = o.astype(BF16)


def _attention(qv, kt, bias_tiles, extra, *, moba):
    B, S, _ = qv.shape
    col0, krow0 = (8, 4) if moba else (0, 0)
    tq = tk = MOBA_BLOCK
    nq = S // tq
    nblk = S // MOBA_BLOCK
    kern = functools.partial(_attn_kernel, moba=moba, tq=tq, tk=tk, nblk=nblk)
    in_specs = [
        pl.BlockSpec((1, tq, HEAD_COLS), lambda b, h, i: (b, i, col0 + h)),
        pl.BlockSpec((1, HEAD_COLS, S), lambda b, h, i: (b, krow0 + h, 0)),
        pl.BlockSpec((1, S, HEAD_COLS), lambda b, h, i: (b, 0, col0 + 4 + h)),
        pl.BlockSpec((1, 2, 2, tq, tk), lambda b, h, i: (h, 0, 0, 0, 0)),
    ]
    if moba:
        (km,) = extra
        in_specs.append(pl.BlockSpec((1, HEAD_COLS, HEAD_COLS), lambda b, h, i: (b, 0, h)))
        args = (qv, kt, qv, bias_tiles, km)
    else:
        lam_p, subln_g = extra
        in_specs.append(pl.BlockSpec(lam_p.shape, lambda b, h, i: (0, 0)))
        in_specs.append(pl.BlockSpec((1, HEAD_COLS), lambda b, h, i: (0, 0)))
        args = (qv, kt, qv, bias_tiles, lam_p, subln_g.reshape(1, HEAD_COLS))
    scratch = [pltpu.VMEM((2 * tq, 2 * HEAD_COLS if moba else HEAD_COLS), BF16),
               pltpu.VMEM((2, 2 * tq, tk), F32),
               pltpu.VMEM((2, 2 * tq, tk), BF16),
               pltpu.VMEM((2, tq, HEAD_COLS), F32),
               pltpu.VMEM((2, tq, HEAD_COLS), F32),
               pltpu.VMEM((2, tq, HEAD_COLS), F32)]
    return pl.pallas_call(
        kern,
        grid=(B, 4, nq),
        in_specs=in_specs,
        out_specs=pl.BlockSpec((1, tq, HEAD_COLS), lambda b, h, i: (b, i, h)),
        out_shape=jax.ShapeDtypeStruct((B, S, 4 * HEAD_COLS), BF16),
        scratch_shapes=scratch,
        compiler_params=_params(("parallel", "parallel", "arbitrary")),
        name="moba_attn" if moba else "diff_attn",
    )(*args)


def _pad_kmean(kmean):
    return jnp.pad(kmean, ((0, 0), (0, HEAD_COLS - kmean.shape[1]), (0, 0)))


def _t5_bucket(n):
    max_exact = REL_BUCKETS // 2
    nf = jnp.maximum(n, 1).astype(F32)
    large = max_exact + (jnp.log(nf / max_exact) / math.log(REL_MAX_DIST / max_exact)
                         * (REL_BUCKETS - max_exact)).astype(jnp.int32)
    large = jnp.minimum(large, REL_BUCKETS - 1)
    return jnp.where(n < max_exact, n, large)


def _bias_tiles(rel_bias, S, t):
    assert REL_MAX_DIST <= t and 2 * t <= S
    by_dist = rel_bias[_t5_bucket(jnp.arange(S))] - rel_bias[REL_BUCKETS - 1]
    m = jnp.arange(2 * t)
    d = jnp.where(m < t, -m, 2 * t - m)
    tiles = []
    for kind in range(2):
        w = by_dist[jnp.clip(kind * t + d, 0, S - 1)]
        if kind == 0:
            w = jnp.where((d >= 0)[:, None], w, NEG_INF)
        skew = jnp.tile(w.T, (1, t))[:, :t * (2 * t - 1)].reshape(-1, t, 2 * t - 1)
        tiles.append(skew[:, :, :t])
    tiles = jnp.stack(tiles, axis=1)
    diff = tiles[:2 * DIFF_HEADS].reshape(2, DIFF_HEADS, 2, t, t).transpose(1, 0, 2, 3, 4)
    moba = tiles[2 * DIFF_HEADS:].reshape(MOBA_HEADS // 2, 2, 2, t, t)
    return diff, moba


def _out_kernel(d_ref, m_ref, wd_ref, wm_ref, x_ref, g1_ref, ng_ref, sc_ref, sh_ref, x1_ref, h2_ref):
    mix = (jnp.dot(d_ref[0], wd_ref[...], preferred_element_type=F32)
           + jnp.dot(m_ref[0], wm_ref[...], preferred_element_type=F32))
    x1 = x_ref[0] + g1_ref[0] * mix
    x1_ref[0] = x1
    h2_ref[0] = _rms_modulate(x1, ng_ref[...], sc_ref[0], sh_ref[0]).astype(BF16)


def _out_proj(d_out, m_out, w_out_bf16, x, g1, norm_g, sc, sh, tm=512):
    B, S, D = x.shape
    half = d_out.shape[-1]
    vec = pl.BlockSpec((1, 1, D), lambda b, i: (b, 0, 0))
    tok = lambda w: pl.BlockSpec((1, tm, w), lambda b, i: (b, i, 0))
    return pl.pallas_call(
        _out_kernel,
        grid=(B, S // tm),
        in_specs=[tok(half), tok(half),
                  pl.BlockSpec((half, D), lambda b, i: (0, 0)),
                  pl.BlockSpec((half, D), lambda b, i: (1, 0)),
                  tok(D), vec,
                  pl.BlockSpec((1, D), lambda b, i: (0, 0)),
                  vec, vec],
        out_specs=[tok(D), tok(D)],
        out_shape=[jax.ShapeDtypeStruct((B, S, D), F32), jax.ShapeDtypeStruct((B, S, D), BF16)],
        compiler_params=_params(("parallel", "arbitrary")),
        name="out_proj",
    )(d_out, m_out, w_out_bf16, w_out_bf16, x, g1.reshape(B, 1, D), norm_g.reshape(1, D),
      sc.reshape(B, 1, D), sh.reshape(B, 1, D))


def _top_values(w, k):
    row = lax.broadcasted_iota(jnp.int32, (k, w.shape[1]), 0)
    out = jnp.zeros((k, w.shape[1]), F32)
    for r in range(k):
        m = jnp.max(w, axis=0, keepdims=True)
        out = jnp.where(row == r, m, out)
        w = jnp.where(w == m, -jnp.inf, w)
    return out


def _gelu_exact(a):
    return 0.5 * a * (1.0 + lax.erf(a * math.sqrt(0.5)))


def _peer_kernel(h2_ref, x1_ref, g2_ref, wqt_ref, keys_ref, u_ref, vt_ref, fg_ref, o_ref,
                 h2t_s, e1_s, e2_s, fl_s, act_s, p_s, acc_s, *, tm, ce, n_chunks):
    s = pl.program_id(1)
    nk = PEER_NKEYS

    @pl.when(s == 0)
    def _select():
        h2t_s[...] = h2_ref[...].astype(F32).T.astype(BF16)
        qt = jnp.dot(wqt_ref[...], h2t_s[...], preferred_element_type=F32)
        tl = 256
        inf = jnp.full((1, tl), jnp.inf, F32)
        for h in range(PEER_HEADS):
            sc = []
            for c in range(2):
                hc = 2 * h + c
                q = qt[hc * PEER_KEY_DIM:(hc + 1) * PEER_KEY_DIM].astype(BF16)
                sc.append(jnp.dot(keys_ref[hc], q, preferred_element_type=F32))
            for t0 in range(0, tm, tl):
                cols = slice(t0, t0 + tl)
                s1, s2 = sc[0][:, cols], sc[1][:, cols]
                v1, v2 = _top_values(s1, PEER_TOPK), _top_values(s2, PEER_TOPK)
                cand = [v1[a:a + 1] + v2 for a in range(PEER_TOPK)]
                tau = _top_values(jnp.concatenate(cand, axis=0), PEER_TOPK)[PEER_TOPK - 1:PEER_TOPK]
                peak = v1[0:1] + v2[0:1]
                z = jnp.zeros((1, tl), F32)
                th = jnp.broadcast_to(inf, (nk, tl))
                for a in range(PEER_TOPK):
                    keep = cand[a] >= tau
                    z = z + jnp.sum(jnp.where(keep, jnp.exp(cand[a] - peak), 0.0), axis=0, keepdims=True)
                    th_a = jnp.min(jnp.where(keep, v2, jnp.inf), axis=0, keepdims=True)
                    th = jnp.where(s1 == v1[a:a + 1], th_a, th)
                fl_s[h, :, cols] = jnp.exp(th - v2[0:1])
                e1_s[h, :, cols] = jnp.where(s1 >= v1[PEER_TOPK - 1:], jnp.exp(s1 - v1[0:1]), 0.0) / z
                e2_s[h, :, cols] = jnp.where(s2 >= v2[PEER_TOPK - 1:], jnp.exp(s2 - v2[0:1]), 0.0)
        act_s[...] = jnp.zeros(act_s.shape, F32)
        p_s[...] = jnp.zeros(p_s.shape, BF16)
        acc_s[...] = jnp.zeros(acc_s.shape, F32)

    chunk = jnp.clip(s - 1, 0, n_chunks - 1)
    tl = 256
    for t0 in range(0, tm, tl):
        cols = slice(t0, t0 + tl)
        acc_s[:, cols] += jnp.dot(vt_ref[0], p_s[:, cols], preferred_element_type=F32)
        for ii in range(ce // nk):
            i = chunk * (ce // nk) + ii
            g = jnp.zeros((nk, tl), F32)
            for h in range(PEER_HEADS):
                floor_row = fl_s[h, pl.ds(i, 1), cols]
                e1_row = e1_s[h, pl.ds(i, 1), cols]
                e2 = e2_s[h, :, cols]
                g = g + e1_row * jnp.where(e2 >= floor_row, e2, 0.0)
            rows = slice(ii * nk, (ii + 1) * nk)
            p_s[rows, cols] = (g * _gelu_exact(act_s[rows, cols])).astype(BF16)
        act_s[:, cols] = jnp.dot(u_ref[...], h2t_s[:, cols], preferred_element_type=F32)

    @pl.when(s == pl.num_programs(1) - 1)
    def _fin():
        y = x1_ref[...] + g2_ref[0] * acc_s[...].T
        ms = jnp.mean(y * y, axis=-1, keepdims=True)
        o_ref[...] = y * lax.rsqrt(ms + NORM_EPS) * fg_ref[...]


def _peer(h2, x1, g2, wqt, keys, u, vt, final_g, S, tm=512, ce=512):
    N, D = h2.shape
    n_chunks = u.shape[0] // ce
    vt = vt.reshape(D, n_chunks, ce).transpose(1, 0, 2)
    B = g2.shape[0]
    kern = functools.partial(_peer_kernel, tm=tm, ce=ce, n_chunks=n_chunks)
    head_rows = pltpu.VMEM((PEER_HEADS, PEER_NKEYS, tm), F32)
    return pl.pallas_call(
        kern,
        grid=(N // tm, n_chunks + 2),
        in_specs=[pl.BlockSpec((tm, D), lambda t, s: (t, 0)),
                  pl.BlockSpec((tm, D), lambda t, s: (t, 0)),
                  pl.BlockSpec((1, 1, D), lambda t, s: ((t * tm) // S, 0, 0)),
                  pl.BlockSpec(wqt.shape, lambda t, s: (0, 0)),
                  pl.BlockSpec(keys.shape, lambda t, s: (0, 0, 0)),
                  pl.BlockSpec((ce, D), lambda t, s: (jnp.minimum(s, n_chunks - 1), 0)),
                  pl.BlockSpec((1, D, ce), lambda t, s: (jnp.clip(s - 2, 0, n_chunks - 1), 0, 0)),
                  pl.BlockSpec((1, D), lambda t, s: (0, 0))],
        out_specs=pl.BlockSpec((tm, D), lambda t, s: (t, 0)),
        out_shape=jax.ShapeDtypeStruct((N, D), F32),
        scratch_shapes=[pltpu.VMEM((D, tm), BF16),
                        head_rows, head_rows, head_rows,
                        pltpu.VMEM((ce, tm), F32),
                        pltpu.VMEM((ce, tm), BF16),
                        pltpu.VMEM((D, tm), F32)],
        compiler_params=_params(("parallel", "arbitrary")),
        name="peer",
    )(h2, x1, g2.reshape(B, 1, D), wqt, keys, u, vt, final_g.reshape(1, D))


def kernel(x, c, rel_bias, w_ada, b_ada, norm1_g, w_in, diff_lambda, diff_subln_g, w_out, norm2_g,
           peer_wq, peer_keys, peer_u, peer_v, final_g):
    B, S, D = x.shape
    assert w_ada.shape[0] == 1, "single-layer kernel"
    assert S % MOBA_BLOCK == 0 and D == D_MODEL
    diff_tiles, moba_tiles = _bias_tiles(rel_bias, S, MOBA_BLOCK)

    mod = _modulation(c, w_ada[0], b_ada[0])
    sh1, sc1, g1, sh2, sc2, g2 = jnp.split(mod, 6, axis=-1)

    w = w_in[0].astype(BF16)
    w_qv = jnp.concatenate([w[:, 0:512], w[:, 1024:2048], w[:, 2560:3072]], axis=1)
    w_k = jnp.concatenate([w[:, 512:1024], w[:, 2048:2560]], axis=1)
    qv, kt, kmean = _in_proj(x, norm1_g[0], sc1, sh1, w_qv, w_k)
    km = _pad_kmean(kmean.reshape(B, S // MOBA_BLOCK, 512))

    d_out = _attention(qv, kt, diff_tiles, (diff_lambda[0], diff_subln_g[0]), moba=False)
    m_out = _attention(qv, kt, moba_tiles, (km,), moba=True)

    x1, h2 = _out_proj(d_out, m_out, w_out[0].astype(BF16), x, g1, norm2_g[0], sc2, sh2)

    wqt = peer_wq[0].T.astype(BF16)
    keys = peer_keys[0].reshape(2 * PEER_HEADS, PEER_NKEYS, PEER_KEY_DIM).astype(BF16)
    out = _peer(h2.reshape(B * S, D), x1.reshape(B * S, D), g2, wqt, keys,
                peer_u[0].astype(BF16), peer_v[0].T.astype(BF16), final_g, S)
    return out.reshape(B, S, D)
```

```python
import functools
import math

import jax
import jax.numpy as jnp
from jax import lax
from jax.experimental import pallas as pl
from jax.experimental.pallas import tpu as pltpu

F32 = jnp.float32
BF16 = jnp.bfloat16

D_MODEL = 1024
DIFF_HEADS = 4
MOBA_HEADS = 8
HEAD_COLS = 128
HALF = 64
MOBA_BLOCK = 256
MOBA_TOPK = 3
QV_COLS = 2048
K_COLS = 1024
REL_BUCKETS = 32
REL_MAX_DIST = 128
PEER_HEADS = 8
PEER_NKEYS = 128
PEER_KEY_DIM = 128
PEER_TOPK = 16
NORM_EPS = 1e-6
NEG_INF = -1e30
LAMBDA_INIT = 0.8 - 0.6 * math.exp(-0.3 * 0)

VMEM_LIMIT_BYTES = 56 * 1024 * 1024


def _params(semantics):
    return pltpu.CompilerParams(dimension_semantics=semantics, vmem_limit_bytes=VMEM_LIMIT_BYTES)


def _mod_kernel(c_ref, w_ref, b_ref, o_ref):
    c = c_ref[...]
    act = c * jax.nn.sigmoid(c)
    o_ref[...] = jnp.dot(act, w_ref[...], preferred_element_type=F32,
                         precision=lax.Precision.HIGHEST) + b_ref[...]


def _modulation(c, w_ada, b_ada):
    B, D = c.shape
    n_out = w_ada.shape[1]
    tn = 1024
    return pl.pallas_call(
        _mod_kernel,
        grid=(n_out // tn,),
        in_specs=[pl.BlockSpec((B, D), lambda n: (0, 0)),
                  pl.BlockSpec((D, tn), lambda n: (0, n)),
                  pl.BlockSpec((1, tn), lambda n: (0, n))],
        out_specs=pl.BlockSpec((B, tn), lambda n: (0, n)),
        out_shape=jax.ShapeDtypeStruct((B, n_out), F32),
        compiler_params=_params(("arbitrary",)),
        name="mod",
    )(c, w_ada, b_ada.reshape(1, n_out))


def _rms_modulate(x, g, sc, sh):
    ms = jnp.mean(x * x, axis=-1, keepdims=True)
    return (x * lax.rsqrt(ms + NORM_EPS) * g) * (1.0 + sc) + sh


def _in_kernel(x_ref, g_ref, sc_ref, sh_ref, wqv_ref, wk_ref, qv_ref, kt_ref, kmean_ref, *, tm, cn):
    h = _rms_modulate(x_ref[0], g_ref[...], sc_ref[0], sh_ref[0]).astype(BF16)
    for n in range(QV_COLS // cn):
        cols = slice(n * cn, (n + 1) * cn)
        qv_ref[0, :, cols] = jnp.dot(h, wqv_ref[:, cols], preferred_element_type=F32).astype(BF16)
    for n in range(K_COLS // cn):
        cols = slice(n * cn, (n + 1) * cn)
        k = jnp.dot(h, wk_ref[:, cols], preferred_element_type=F32)
        kt_ref[0, cols, :] = k.T.astype(BF16)
        if n == 1:
            for r in range(tm // MOBA_BLOCK):
                kmean_ref[0, r] = jnp.mean(k[r * MOBA_BLOCK:(r + 1) * MOBA_BLOCK], axis=0, keepdims=True)


def _in_proj(x, g, sc, sh, w_qv, w_k, tm=512):
    B, S, D = x.shape
    cn = 512
    nblk = S // MOBA_BLOCK
    kern = functools.partial(_in_kernel, tm=tm, cn=cn)
    vec = pl.BlockSpec((1, 1, D), lambda b, i: (b, 0, 0))
    return pl.pallas_call(
        kern,
        grid=(B, S // tm),
        in_specs=[pl.BlockSpec((1, tm, D), lambda b, i: (b, i, 0)),
                  pl.BlockSpec((1, D), lambda b, i: (0, 0)),
                  vec, vec,
                  pl.BlockSpec((D, QV_COLS), lambda b, i: (0, 0)),
                  pl.BlockSpec((D, K_COLS), lambda b, i: (0, 0))],
        out_specs=[pl.BlockSpec((1, tm, QV_COLS), lambda b, i: (b, i, 0)),
                   pl.BlockSpec((1, K_COLS, tm), lambda b, i: (b, 0, i)),
                   pl.BlockSpec((1, tm // MOBA_BLOCK, 1, 512), lambda b, i: (b, i, 0, 0))],
        out_shape=[jax.ShapeDtypeStruct((B, S, QV_COLS), BF16),
                   jax.ShapeDtypeStruct((B, K_COLS, S), BF16),
                   jax.ShapeDtypeStruct((B, nblk, 1, 512), F32)],
        compiler_params=_params(("parallel", "arbitrary")),
        name="in_proj",
    )(x, g.reshape(1, D), sc.reshape(B, 1, D), sh.reshape(B, 1, D), w_qv, w_k)


def _attn_kernel(*refs, moba, tq, tk, nblk, nh):
    if moba:
        q_ref, kt_ref, v_ref, bias_ref, km_ref, o_ref, q2_s, s_s, p_s, m_s, l_s, acc_s = refs
    else:
        q_ref, kt_ref, v_ref, bias_ref, lam_ref, g_ref, o_ref, q2_s, s_s, p_s, m_s, l_s, acc_s = refs
    i = pl.program_id(2)
    lane = lax.broadcasted_iota(jnp.int32, (tq, HEAD_COLS), 1)
    reps = tk // HEAD_COLS

    def hcols(hb):
        return slice(hb * HEAD_COLS, (hb + 1) * HEAD_COLS)

    for hb in range(nh):
        qs = q_ref[0, :, hcols(hb)] * 0.125
        q_lo = jnp.where(lane < HALF, qs, jnp.zeros_like(qs))
        q_hi = jnp.where(lane >= HALF, qs, jnp.zeros_like(qs))
        q2_s[hb, 0:tq, 0:HEAD_COLS] = q_lo
        q2_s[hb, tq:2 * tq, 0:HEAD_COLS] = q_hi
        if moba:
            q2 = jnp.concatenate([q_lo, q_hi], axis=0).astype(F32)
            gate = lax.dot_general(km_ref[0, :, hcols(hb)], q2, (((1,), (1,)), ((), ())),
                                   preferred_element_type=F32,
                                   precision=lax.Precision.HIGHEST)
            nb8 = -(-nblk // 8) * 8
            gate = gate[0:nb8]
            blk = lax.broadcasted_iota(jnp.int32, (nb8, 2 * tq), 0)
            cnt = jnp.zeros((nb8, 2 * tq), jnp.int32)
            for jp in range(nblk):
                row = gate[jp:jp + 1]
                beats = (row > gate) | ((row == gate) & (blk > jp))
                cnt = cnt + jnp.where(beats, jnp.where(i > jp, 1, 0), 0)
            sel = ((blk < i) & (cnt < MOBA_TOPK)) | (blk == i)
            sb = jnp.where(sel, 0.0, NEG_INF).astype(F32)
            sb = jnp.concatenate([sb, jnp.zeros((HEAD_COLS - nb8, 2 * tq), F32)], axis=0)
            q2_s[hb, :, HEAD_COLS:2 * HEAD_COLS] = sb.T.astype(BF16)

    def tile_rows(j):
        return pl.ds(pl.multiple_of(j * tk, tk), tk)

    def qk(hb, j):
        kt = kt_ref[0, hcols(hb), tile_rows(j)]
        if moba:
            row_k = lax.broadcasted_iota(jnp.int32, (HEAD_COLS, tk), 0)
            kt = jnp.concatenate([kt, jnp.where(row_k == j, 1.0, 0.0).astype(BF16)], axis=0)
        return jnp.dot(q2_s[hb], kt, preferred_element_type=F32)

    def stage(j, kind, with_qk):
        cur = j % 2
        nxt = 1 - cur
        for hb in range(nh):
            pv = jnp.dot(p_s[2 * hb + nxt], v_ref[0, tile_rows(jnp.maximum(j - 1, 0)), hcols(hb)],
                         preferred_element_type=F32)
            s = s_s[2 * hb + cur]
            for m in range(2):
                hm = 2 * hb + m
                half = slice(m * tq, (m + 1) * tq)
                sm = s[half]
                if kind is not None:
                    sm = sm + bias_ref[hb, m, kind]
                m_prev = m_s[hm]
                m_new = jnp.maximum(m_prev, jnp.max(sm, axis=1, keepdims=True))
                alpha = jnp.exp(m_prev - m_new)
                p = jnp.exp(sm - jnp.concatenate([m_new] * reps, axis=1))
                psum = p[:, 0:HEAD_COLS]
                for r in range(1, reps):
                    psum = psum + p[:, r * HEAD_COLS:(r + 1) * HEAD_COLS]
                l_s[hm] = alpha * l_s[hm] + psum
                m_s[hm] = m_new
                acc_s[hm] = alpha * (acc_s[hm] + pv[half])
                p_s[2 * hb + cur, half] = p.astype(BF16)
            if with_qk:
                s_s[2 * hb + nxt] = qk(hb, j + 1)

    m_s[...] = jnp.full(m_s.shape, -jnp.inf, F32)
    l_s[...] = jnp.zeros(l_s.shape, F32)
    acc_s[...] = jnp.zeros(acc_s.shape, F32)
    for hb in range(nh):
        p_s[2 * hb + 1] = jnp.zeros(p_s.shape[1:], BF16)
        s_s[2 * hb] = qk(hb, 0)

    def far(j, carry):
        stage(j, None, True)
        return carry

    lax.fori_loop(0, jnp.maximum(i - 1, 0), far, 0)

    @pl.when(i >= 1)
    def _near():
        stage(i - 1, 1, True)

    stage(i, 0, False)
    for hb in range(nh):
        pv = jnp.dot(p_s[2 * hb + i % 2], v_ref[0, tile_rows(i), hcols(hb)], preferred_element_type=F32)
        o0 = (acc_s[2 * hb] + pv[0:tq]) / jnp.sum(l_s[2 * hb], axis=1, keepdims=True)
        o1 = (acc_s[2 * hb + 1] + pv[tq:2 * tq]) / jnp.sum(l_s[2 * hb + 1], axis=1, keepdims=True)
        if moba:
            o = jnp.where(lane < HALF, o0, o1)
        else:
            lp = lam_ref[...]
            lam = (jnp.exp(jnp.sum(lp[0:1] * lp[1:2], axis=1, keepdims=True))
                   - jnp.exp(jnp.sum(lp[2:3] * lp[3:4], axis=1, keepdims=True)) + LAMBDA_INIT)
            d = o0 - lam * o1
            ms = jnp.mean(d * d, axis=-1, keepdims=True)
            o = (d * lax.rsqrt(ms + NORM_EPS) * g_ref[...]) * (1.0 - LAMBDA_INIT)
        o_ref[0, :, hcols(hb)] = o.astype(BF16)


def _attention(qv, kt, bias_tiles, extra, *, moba, nh=1):
    B, S, _ = qv.shape
    col0, krow0 = (8, 4) if moba else (0, 0)
    tq = tk = MOBA_BLOCK
    nq = S // tq
    nblk = S // MOBA_BLOCK
    w = nh * HEAD_COLS
    kern = functools.partial(_attn_kernel, moba=moba, tq=tq, tk=tk, nblk=nblk, nh=nh)
    in_specs = [
        pl.BlockSpec((1, tq, w), lambda b, h, i: (b, i, col0 // nh + h)),
        pl.BlockSpec((1, w, S), lambda b, h, i: (b, krow0 // nh + h, 0)),
        pl.BlockSpec((1, S, w), lambda b, h, i: (b, 0, (col0 + 4) // nh + h)),
        pl.BlockSpec((nh, 2, 2, tq, tk), lambda b, h, i: (h, 0, 0, 0, 0)),
    ]
    if moba:
        (km,) = extra
        in_specs.append(pl.BlockSpec((1, HEAD_COLS, w), lambda b, h, i: (b, 0, h)))
        args = (qv, kt, qv, bias_tiles, km)
    else:
        lam_p, subln_g = extra
        in_specs.append(pl.BlockSpec(lam_p.shape, lambda b, h, i: (0, 0)))
        in_specs.append(pl.BlockSpec((1, HEAD_COLS), lambda b, h, i: (0, 0)))
        args = (qv, kt, qv, bias_tiles, lam_p, subln_g.reshape(1, HEAD_COLS))
    scratch = [pltpu.VMEM((nh, 2 * tq, 2 * HEAD_COLS if moba else HEAD_COLS), BF16),
               pltpu.VMEM((2 * nh, 2 * tq, tk), F32),
               pltpu.VMEM((2 * nh, 2 * tq, tk), BF16),
               pltpu.VMEM((2 * nh, tq, HEAD_COLS), F32),
               pltpu.VMEM((2 * nh, tq, HEAD_COLS), F32),
               pltpu.VMEM((2 * nh, tq, HEAD_COLS), F32)]
    return pl.pallas_call(
        kern,
        grid=(B, 4 // nh, nq),
        in_specs=in_specs,
        out_specs=pl.BlockSpec((1, tq, w), lambda b, h, i: (b, i, h)),
        out_shape=jax.ShapeDtypeStruct((B, S, 4 * HEAD_COLS), BF16),
        scratch_shapes=scratch,
        compiler_params=_params(("parallel", "parallel", "arbitrary")),
        name="moba_attn" if moba else "diff_attn",
    )(*args)


def _pad_kmean(kmean):
    return jnp.pad(kmean, ((0, 0), (0, HEAD_COLS - kmean.shape[1]), (0, 0)))


def _t5_bucket(n):
    max_exact = REL_BUCKETS // 2
    nf = jnp.maximum(n, 1).astype(F32)
    large = max_exact + (jnp.log(nf / max_exact) / math.log(REL_MAX_DIST / max_exact)
                         * (REL_BUCKETS - max_exact)).astype(jnp.int32)
    large = jnp.minimum(large, REL_BUCKETS - 1)
    return jnp.where(n < max_exact, n, large)


def _bias_tiles(rel_bias, S, t):
    assert REL_MAX_DIST <= t and 2 * t <= S
    by_dist = rel_bias[_t5_bucket(jnp.arange(S))] - rel_bias[REL_BUCKETS - 1]
    m = jnp.arange(2 * t)
    d = jnp.where(m < t, -m, 2 * t - m)
    tiles = []
    for kind in range(2):
        w = by_dist[jnp.clip(kind * t + d, 0, S - 1)]
        if kind == 0:
            w = jnp.where((d >= 0)[:, None], w, NEG_INF)
        skew = jnp.tile(w.T, (1, t))[:, :t * (2 * t - 1)].reshape(-1, t, 2 * t - 1)
        tiles.append(skew[:, :, :t])
    tiles = jnp.stack(tiles, axis=1)
    diff = tiles[:2 * DIFF_HEADS].reshape(2, DIFF_HEADS, 2, t, t).transpose(1, 0, 2, 3, 4)
    moba = tiles[2 * DIFF_HEADS:].reshape(MOBA_HEADS // 2, 2, 2, t, t)
    return diff, moba


def _out_kernel(d_ref, m_ref, wd_ref, wm_ref, x_ref, g1_ref, ng_ref, sc_ref, sh_ref, x1_ref, h2_ref):
    mix = (jnp.dot(d_ref[0], wd_ref[...], preferred_element_type=F32)
           + jnp.dot(m_ref[0], wm_ref[...], preferred_element_type=F32))
    x1 = x_ref[0] + g1_ref[0] * mix
    x1_ref[0] = x1
    h2_ref[0] = _rms_modulate(x1, ng_ref[...], sc_ref[0], sh_ref[0]).astype(BF16)


def _out_proj(d_out, m_out, w_out_bf16, x, g1, norm_g, sc, sh, tm=512):
    B, S, D = x.shape
    half = d_out.shape[-1]
    vec = pl.BlockSpec((1, 1, D), lambda b, i: (b, 0, 0))
    tok = lambda w: pl.BlockSpec((1, tm, w), lambda b, i: (b, i, 0))
    return pl.pallas_call(
        _out_kernel,
        grid=(B, S // tm),
        in_specs=[tok(half), tok(half),
                  pl.BlockSpec((half, D), lambda b, i: (0, 0)),
                  pl.BlockSpec((half, D), lambda b, i: (1, 0)),
                  tok(D), vec,
                  pl.BlockSpec((1, D), lambda b, i: (0, 0)),
                  vec, vec],
        out_specs=[tok(D), tok(D)],
        out_shape=[jax.ShapeDtypeStruct((B, S, D), F32), jax.ShapeDtypeStruct((B, S, D), BF16)],
        compiler_params=_params(("parallel", "arbitrary")),
        name="out_proj",
    )(d_out, m_out, w_out_bf16, w_out_bf16, x, g1.reshape(B, 1, D), norm_g.reshape(1, D),
      sc.reshape(B, 1, D), sh.reshape(B, 1, D))


def _top_values(w, k):
    row = lax.broadcasted_iota(jnp.int32, (k, w.shape[1]), 0)
    out = jnp.zeros((k, w.shape[1]), F32)
    for r in range(k):
        m = jnp.max(w, axis=0, keepdims=True)
        out = jnp.where(row == r, m, out)
        w = jnp.where(w == m, -jnp.inf, w)
    return out


def _gelu_exact(a):
    return 0.5 * a * (1.0 + lax.erf(a * math.sqrt(0.5)))


def _peer_kernel(h2_ref, x1_ref, g2_ref, wqt_ref, keys_ref, u_ref, vt_ref, fg_ref, o_ref,
                 h2t_s, e1_s, e2_s, fl_s, act_s, p_s, acc_s, *, tm, ce, n_chunks):
    s = pl.program_id(1)
    nk = PEER_NKEYS

    @pl.when(s == 0)
    def _select():
        h2t_s[...] = h2_ref[...].astype(F32).T.astype(BF16)
        qt = jnp.dot(wqt_ref[...], h2t_s[...], preferred_element_type=F32)
        tl = 256
        inf = jnp.full((1, tl), jnp.inf, F32)
        for h in range(PEER_HEADS):
            sc = []
            for c in range(2):
                hc = 2 * h + c
                q = qt[hc * PEER_KEY_DIM:(hc + 1) * PEER_KEY_DIM].astype(BF16)
                sc.append(jnp.dot(keys_ref[hc], q, preferred_element_type=F32))
            for t0 in range(0, tm, tl):
                cols = slice(t0, t0 + tl)
                s1, s2 = sc[0][:, cols], sc[1][:, cols]
                v1, v2 = _top_values(s1, PEER_TOPK), _top_values(s2, PEER_TOPK)
                cand = [v1[a:a + 1] + v2 for a in range(PEER_TOPK)]
                tau = _top_values(jnp.concatenate(cand, axis=0), PEER_TOPK)[PEER_TOPK - 1:PEER_TOPK]
                peak = v1[0:1] + v2[0:1]
                z = jnp.zeros((1, tl), F32)
                th = jnp.broadcast_to(inf, (nk, tl))
                for a in range(PEER_TOPK):
                    keep = cand[a] >= tau
                    z = z + jnp.sum(jnp.where(keep, jnp.exp(cand[a] - peak), 0.0), axis=0, keepdims=True)
                    th_a = jnp.min(jnp.where(keep, v2, jnp.inf), axis=0, keepdims=True)
                    th = jnp.where(s1 == v1[a:a + 1], th_a, th)
                fl_s[h, :, cols] = jnp.exp(th - v2[0:1])
                e1_s[h, :, cols] = jnp.where(s1 >= v1[PEER_TOPK - 1:], jnp.exp(s1 - v1[0:1]), 0.0) / z
                e2_s[h, :, cols] = jnp.where(s2 >= v2[PEER_TOPK - 1:], jnp.exp(s2 - v2[0:1]), 0.0)
        act_s[...] = jnp.zeros(act_s.shape, F32)
        p_s[...] = jnp.zeros(p_s.shape, BF16)
        acc_s[...] = jnp.zeros(acc_s.shape, F32)

    act_s[...] = jnp.dot(u_ref[...], h2t_s[...], preferred_element_type=F32)
    for ii in range(ce // nk):
        i = s * (ce // nk) + ii
        g = jnp.zeros((nk, tm), F32)
        for h in range(PEER_HEADS):
            floor_row = fl_s[h, pl.ds(i, 1), :]
            e1_row = e1_s[h, pl.ds(i, 1), :]
            e2 = e2_s[h]
            g = g + e1_row * jnp.where(e2 >= floor_row, e2, 0.0)
        rows = slice(ii * nk, (ii + 1) * nk)
        p_s[rows] = (g * _gelu_exact(act_s[rows])).astype(BF16)
    acc_s[...] += jnp.dot(vt_ref[0], p_s[...], preferred_element_type=F32)

    @pl.when(s == pl.num_programs(1) - 1)
    def _fin():
        y = x1_ref[...] + g2_ref[0] * acc_s[...].T
        ms = jnp.mean(y * y, axis=-1, keepdims=True)
        o_ref[...] = y * lax.rsqrt(ms + NORM_EPS) * fg_ref[...]


def _peer(h2, x1, g2, wqt, keys, u, vt, final_g, S, tm=512, ce=512):
    N, D = h2.shape
    n_chunks = u.shape[0] // ce
    vt = vt.reshape(D, n_chunks, ce).transpose(1, 0, 2)
    B = g2.shape[0]
    kern = functools.partial(_peer_kernel, tm=tm, ce=ce, n_chunks=n_chunks)
    head_rows = pltpu.VMEM((PEER_HEADS, PEER_NKEYS, tm), F32)
    return pl.pallas_call(
        kern,
        grid=(N // tm, n_chunks),
        in_specs=[pl.BlockSpec((tm, D), lambda t, s: (t, 0)),
                  pl.BlockSpec((tm, D), lambda t, s: (t, 0)),
                  pl.BlockSpec((1, 1, D), lambda t, s: ((t * tm) // S, 0, 0)),
                  pl.BlockSpec(wqt.shape, lambda t, s: (0, 0)),
                  pl.BlockSpec(keys.shape, lambda t, s: (0, 0, 0)),
                  pl.BlockSpec((ce, D), lambda t, s: (s, 0)),
                  pl.BlockSpec((1, D, ce), lambda t, s: (s, 0, 0)),
                  pl.BlockSpec((1, D), lambda t, s: (0, 0))],
        out_specs=pl.BlockSpec((tm, D), lambda t, s: (t, 0)),
        out_shape=jax.ShapeDtypeStruct((N, D), F32),
        scratch_shapes=[pltpu.VMEM((D, tm), BF16),
                        head_rows, head_rows, head_rows,
                        pltpu.VMEM((ce, tm), F32),
                        pltpu.VMEM((ce, tm), BF16),
                        pltpu.VMEM((D, tm), F32)],
        compiler_params=_params(("parallel", "arbitrary")),
        name="peer",
    )(h2, x1, g2.reshape(B, 1, D), wqt, keys, u, vt, final_g.reshape(1, D))


def kernel(x, c, rel_bias, w_ada, b_ada, norm1_g, w_in, diff_lambda, diff_subln_g, w_out, norm2_g,
           peer_wq, peer_keys, peer_u, peer_v, final_g):
    B, S, D = x.shape
    assert w_ada.shape[0] == 1, "single-layer kernel"
    assert S % MOBA_BLOCK == 0 and D == D_MODEL
    diff_tiles, moba_tiles = _bias_tiles(rel_bias, S, MOBA_BLOCK)

    mod = _modulation(c, w_ada[0], b_ada[0])
    sh1, sc1, g1, sh2, sc2, g2 = jnp.split(mod, 6, axis=-1)

    w = w_in[0].astype(BF16)
    w_qv = jnp.concatenate([w[:, 0:512], w[:, 1024:2048], w[:, 2560:3072]], axis=1)
    w_k = jnp.concatenate([w[:, 512:1024], w[:, 2048:2560]], axis=1)
    qv, kt, kmean = _in_proj(x, norm1_g[0], sc1, sh1, w_qv, w_k)
    km = _pad_kmean(kmean.reshape(B, S // MOBA_BLOCK, 512))

    d_out = _attention(qv, kt, diff_tiles, (diff_lambda[0], diff_subln_g[0]), moba=False)
    m_out = _attention(qv, kt, moba_tiles, (km,), moba=True)

    x1, h2 = _out_proj(d_out, m_out, w_out[0].astype(BF16), x, g1, norm2_g[0], sc2, sh2)

    wqt = peer_wq[0].T.astype(BF16)
    keys = peer_keys[0].reshape(2 * PEER_HEADS, PEER_NKEYS, PEER_KEY_DIM).astype(BF16)
    out = _peer(h2.reshape(B * S, D), x1.reshape(B * S, D), g2, wqt, keys,
                peer_u[0].astype(BF16), peer_v[0].T.astype(BF16), final_g, S)
    return out.reshape(B, S, D)
```

```python
import functools
import math

import jax
import jax.numpy as jnp
from jax import lax
from jax.experimental import pallas as pl
from jax.experimental.pallas import tpu as pltpu

F32 = jnp.float32
BF16 = jnp.bfloat16

D_MODEL = 1024
DIFF_HEADS = 4
MOBA_HEADS = 8
HEAD_COLS = 128
HALF = 64
MOBA_BLOCK = 256
MOBA_TOPK = 3
QV_COLS = 2048
K_COLS = 1024
REL_BUCKETS = 32
REL_MAX_DIST = 128
PEER_HEADS = 8
PEER_NKEYS = 128
PEER_KEY_DIM = 128
PEER_TOPK = 16
NORM_EPS = 1e-6
NEG_INF = -1e30
LAMBDA_INIT = 0.8 - 0.6 * math.exp(-0.3 * 0)

VMEM_LIMIT_BYTES = 56 * 1024 * 1024


def _params(semantics):
    return pltpu.CompilerParams(dimension_semantics=semantics, vmem_limit_bytes=VMEM_LIMIT_BYTES)


def _mod_kernel(c_ref, w_ref, b_ref, o_ref):
    c = c_ref[...]
    act = c * jax.nn.sigmoid(c)
    o_ref[...] = jnp.dot(act, w_ref[...], preferred_element_type=F32,
                         precision=lax.Precision.HIGHEST) + b_ref[...]


def _modulation(c, w_ada, b_ada):
    B, D = c.shape
    n_out = w_ada.shape[1]
    tn = 1024
    return pl.pallas_call(
        _mod_kernel,
        grid=(n_out // tn,),
        in_specs=[pl.BlockSpec((B, D), lambda n: (0, 0)),
                  pl.BlockSpec((D, tn), lambda n: (0, n)),
                  pl.BlockSpec((1, tn), lambda n: (0, n))],
        out_specs=pl.BlockSpec((B, tn), lambda n: (0, n)),
        out_shape=jax.ShapeDtypeStruct((B, n_out), F32),
        compiler_params=_params(("arbitrary",)),
        name="mod",
    )(c, w_ada, b_ada.reshape(1, n_out))


def _rms_modulate(x, g, sc, sh):
    ms = jnp.mean(x * x, axis=-1, keepdims=True)
    return (x * lax.rsqrt(ms + NORM_EPS) * g) * (1.0 + sc) + sh


def _in_kernel(x_ref, g_ref, sc_ref, sh_ref, wqv_ref, wk_ref, qv_ref, kt_ref, kmean_ref, *, tm, cn):
    h = _rms_modulate(x_ref[0], g_ref[...], sc_ref[0], sh_ref[0]).astype(BF16)
    for n in range(QV_COLS // cn):
        cols = slice(n * cn, (n + 1) * cn)
        qv_ref[0, :, cols] = jnp.dot(h, wqv_ref[:, cols], preferred_element_type=F32).astype(BF16)
    for n in range(K_COLS // cn):
        cols = slice(n * cn, (n + 1) * cn)
        k = jnp.dot(h, wk_ref[:, cols], preferred_element_type=F32)
        kt_ref[0, cols, :] = k.T.astype(BF16)
        if n == 1:
            for r in range(tm // MOBA_BLOCK):
                kmean_ref[0, r] = jnp.mean(k[r * MOBA_BLOCK:(r + 1) * MOBA_BLOCK], axis=0, keepdims=True)


def _in_proj(x, g, sc, sh, w_qv, w_k, tm=512):
    B, S, D = x.shape
    cn = 512
    nblk = S // MOBA_BLOCK
    kern = functools.partial(_in_kernel, tm=tm, cn=cn)
    vec = pl.BlockSpec((1, 1, D), lambda b, i: (b, 0, 0))
    return pl.pallas_call(
        kern,
        grid=(B, S // tm),
        in_specs=[pl.BlockSpec((1, tm, D), lambda b, i: (b, i, 0)),
                  pl.BlockSpec((1, D), lambda b, i: (0, 0)),
                  vec, vec,
                  pl.BlockSpec((D, QV_COLS), lambda b, i: (0, 0)),
                  pl.BlockSpec((D, K_COLS), lambda b, i: (0, 0))],
        out_specs=[pl.BlockSpec((1, tm, QV_COLS), lambda b, i: (b, i, 0)),
                   pl.BlockSpec((1, K_COLS, tm), lambda b, i: (b, 0, i)),
                   pl.BlockSpec((1, tm // MOBA_BLOCK, 1, 512), lambda b, i: (b, i, 0, 0))],
        out_shape=[jax.ShapeDtypeStruct((B, S, QV_COLS), BF16),
                   jax.ShapeDtypeStruct((B, K_COLS, S), BF16),
                   jax.ShapeDtypeStruct((B, nblk, 1, 512), F32)],
        compiler_params=_params(("parallel", "arbitrary")),
        name="in_proj",
    )(x, g.reshape(1, D), sc.reshape(B, 1, D), sh.reshape(B, 1, D), w_qv, w_k)


def _attn_kernel(*refs, moba, tq, tk, nblk, nh):
    if moba:
        q_ref, kt_ref, v_ref, bias_ref, km_ref, o_ref, q2_s, s_s, p_s, m_s, l_s, acc_s = refs
    else:
        q_ref, kt_ref, v_ref, bias_ref, lam_ref, g_ref, o_ref, q2_s, s_s, p_s, m_s, l_s, acc_s = refs
    i = pl.program_id(2)
    lane = lax.broadcasted_iota(jnp.int32, (tq, HEAD_COLS), 1)
    reps = tk // HEAD_COLS

    def hcols(hb):
        return slice(hb * HEAD_COLS, (hb + 1) * HEAD_COLS)

    for hb in range(nh):
        qs = q_ref[0, :, hcols(hb)] * 0.125
        q_lo = jnp.where(lane < HALF, qs, jnp.zeros_like(qs))
        q_hi = jnp.where(lane >= HALF, qs, jnp.zeros_like(qs))
        q2_s[hb, 0:tq, 0:HEAD_COLS] = q_lo
        q2_s[hb, tq:2 * tq, 0:HEAD_COLS] = q_hi
        if moba:
            q2 = jnp.concatenate([q_lo, q_hi], axis=0).astype(F32)
            gate = lax.dot_general(km_ref[0, :, hcols(hb)], q2, (((1,), (1,)), ((), ())),
                                   preferred_element_type=F32,
                                   precision=lax.Precision.HIGHEST)
            nb8 = -(-nblk // 8) * 8
            gate = gate[0:nb8]
            blk = lax.broadcasted_iota(jnp.int32, (nb8, 2 * tq), 0)
            cnt = jnp.zeros((nb8, 2 * tq), jnp.int32)
            for jp in range(nblk):
                row = gate[jp:jp + 1]
                beats = (row > gate) | ((row == gate) & (blk > jp))
                cnt = cnt + jnp.where(beats, jnp.where(i > jp, 1, 0), 0)
            sel = ((blk < i) & (cnt < MOBA_TOPK)) | (blk == i)
            sb = jnp.where(sel, 0.0, NEG_INF).astype(F32)
            sb = jnp.concatenate([sb, jnp.zeros((HEAD_COLS - nb8, 2 * tq), F32)], axis=0)
            q2_s[hb, :, HEAD_COLS:2 * HEAD_COLS] = sb.T.astype(BF16)

    def tile_rows(j):
        return pl.ds(pl.multiple_of(j * tk, tk), tk)

    def qk(hb, j):
        kt = kt_ref[0, hcols(hb), tile_rows(j)]
        if moba:
            row_k = lax.broadcasted_iota(jnp.int32, (HEAD_COLS, tk), 0)
            kt = jnp.concatenate([kt, jnp.where(row_k == j, 1.0, 0.0).astype(BF16)], axis=0)
        return jnp.dot(q2_s[hb], kt, preferred_element_type=F32)

    def stage(j, kind, with_qk):
        cur = j % 2
        nxt = 1 - cur
        for hb in range(nh):
            pv = jnp.dot(p_s[2 * hb + nxt], v_ref[0, tile_rows(jnp.maximum(j - 1, 0)), hcols(hb)],
                         preferred_element_type=F32)
            s = s_s[2 * hb + cur]
            for m in range(2):
                hm = 2 * hb + m
                half = slice(m * tq, (m + 1) * tq)
                sm = s[half]
                if kind is not None:
                    sm = sm + bias_ref[hb, m, kind]
                m_prev = m_s[hm]
                m_new = jnp.maximum(m_prev, jnp.max(sm, axis=1, keepdims=True))
                alpha = jnp.exp(m_prev - m_new)
                p = jnp.exp(sm - jnp.concatenate([m_new] * reps, axis=1))
                psum = p[:, 0:HEAD_COLS]
                for r in range(1, reps):
                    psum = psum + p[:, r * HEAD_COLS:(r + 1) * HEAD_COLS]
                l_s[hm] = alpha * l_s[hm] + psum
                m_s[hm] = m_new
                acc_s[hm] = alpha * (acc_s[hm] + pv[half])
                p_s[2 * hb + cur, half] = p.astype(BF16)
            if with_qk:
                s_s[2 * hb + nxt] = qk(hb, j + 1)

    m_s[...] = jnp.full(m_s.shape, -jnp.inf, F32)
    l_s[...] = jnp.zeros(l_s.shape, F32)
    acc_s[...] = jnp.zeros(acc_s.shape, F32)
    for hb in range(nh):
        p_s[2 * hb + 1] = jnp.zeros(p_s.shape[1:], BF16)
        s_s[2 * hb] = qk(hb, 0)

    def far(j, carry):
        stage(j, None, True)
        return carry

    lax.fori_loop(0, jnp.maximum(i - 1, 0), far, 0)

    @pl.when(i >= 1)
    def _near():
        stage(i - 1, 1, True)

    stage(i, 0, False)
    for hb in range(nh):
        pv = jnp.dot(p_s[2 * hb + i % 2], v_ref[0, tile_rows(i), hcols(hb)], preferred_element_type=F32)
        o0 = (acc_s[2 * hb] + pv[0:tq]) / jnp.sum(l_s[2 * hb], axis=1, keepdims=True)
        o1 = (acc_s[2 * hb + 1] + pv[tq:2 * tq]) / jnp.sum(l_s[2 * hb + 1], axis=1, keepdims=True)
        if moba:
            o = jnp.where(lane < HALF, o0, o1)
        else:
            lp = lam_ref[...]
            lam = (jnp.exp(jnp.sum(lp[0:1] * lp[1:2], axis=1, keepdims=True))
                   - jnp.exp(jnp.sum(lp[2:3] * lp[3:4], axis=1, keepdims=True)) + LAMBDA_INIT)
            d = o0 - lam * o1
            ms = jnp.mean(d * d, axis=-1, keepdims=True)
            o = (d * lax.rsqrt(ms + NORM_EPS) * g_ref[...]) * (1.0 - LAMBDA_INIT)
        o_ref[0, :, hcols(hb)] = o.astype(BF16)


def _attention(qv, kt, bias_tiles, extra, *, moba, nh=1):
    B, S, _ = qv.shape
    col0, krow0 = (8, 4) if moba else (0, 0)
    tq = tk = MOBA_BLOCK
    nq = S // tq
    nblk = S // MOBA_BLOCK
    w = nh * HEAD_COLS
    kern = functools.partial(_attn_kernel, moba=moba, tq=tq, tk=tk, nblk=nblk, nh=nh)
    in_specs = [
        pl.BlockSpec((1, tq, w), lambda b, h, i: (b, i, col0 // nh + h)),
        pl.BlockSpec((1, w, S), lambda b, h, i: (b, krow0 // nh + h, 0)),
        pl.BlockSpec((1, S, w), lambda b, h, i: (b, 0, (col0 + 4) // nh + h)),
        pl.BlockSpec((nh, 2, 2, tq, tk), lambda b, h, i: (h, 0, 0, 0, 0)),
    ]
    if moba:
        (km,) = extra
        in_specs.append(pl.BlockSpec((1, HEAD_COLS, w), lambda b, h, i: (b, 0, h)))
        args = (qv, kt, qv, bias_tiles, km)
    else:
        lam_p, subln_g = extra
        in_specs.append(pl.BlockSpec(lam_p.shape, lambda b, h, i: (0, 0)))
        in_specs.append(pl.BlockSpec((1, HEAD_COLS), lambda b, h, i: (0, 0)))
        args = (qv, kt, qv, bias_tiles, lam_p, subln_g.reshape(1, HEAD_COLS))
    scratch = [pltpu.VMEM((nh, 2 * tq, 2 * HEAD_COLS if moba else HEAD_COLS), BF16),
               pltpu.VMEM((2 * nh, 2 * tq, tk), F32),
               pltpu.VMEM((2 * nh, 2 * tq, tk), BF16),
               pltpu.VMEM((2 * nh, tq, HEAD_COLS), F32),
               pltpu.VMEM((2 * nh, tq, HEAD_COLS), F32),
               pltpu.VMEM((2 * nh, tq, HEAD_COLS), F32)]
    return pl.pallas_call(
        kern,
        grid=(B, 4 // nh, nq),
        in_specs=in_specs,
        out_specs=pl.BlockSpec((1, tq, w), lambda b, h, i: (b, i, h)),
        out_shape=jax.ShapeDtypeStruct((B, S, 4 * HEAD_COLS), BF16),
        scratch_shapes=scratch,
        compiler_params=_params(("parallel", "parallel", "arbitrary")),
        name="moba_attn" if moba else "diff_attn",
    )(*args)


def _pad_kmean(kmean):
    return jnp.pad(kmean, ((0, 0), (0, HEAD_COLS - kmean.shape[1]), (0, 0)))


def _t5_bucket(n):
    max_exact = REL_BUCKETS // 2
    nf = jnp.maximum(n, 1).astype(F32)
    large = max_exact + (jnp.log(nf / max_exact) / math.log(REL_MAX_DIST / max_exact)
                         * (REL_BUCKETS - max_exact)).astype(jnp.int32)
    large = jnp.minimum(large, REL_BUCKETS - 1)
    return jnp.where(n < max_exact, n, large)


def _bias_tiles(rel_bias, S, t):
    assert REL_MAX_DIST <= t and 2 * t <= S
    by_dist = rel_bias[_t5_bucket(jnp.arange(S))] - rel_bias[REL_BUCKETS - 1]
    m = jnp.arange(2 * t)
    d = jnp.where(m < t, -m, 2 * t - m)
    tiles = []
    for kind in range(2):
        w = by_dist[jnp.clip(kind * t + d, 0, S - 1)]
        if kind == 0:
            w = jnp.where((d >= 0)[:, None], w, NEG_INF)
        skew = jnp.tile(w.T, (1, t))[:, :t * (2 * t - 1)].reshape(-1, t, 2 * t - 1)
        tiles.append(skew[:, :, :t])
    tiles = jnp.stack(tiles, axis=1)
    diff = tiles[:2 * DIFF_HEADS].reshape(2, DIFF_HEADS, 2, t, t).transpose(1, 0, 2, 3, 4)
    moba = tiles[2 * DIFF_HEADS:].reshape(MOBA_HEADS // 2, 2, 2, t, t)
    return diff, moba


def _out_kernel(d_ref, m_ref, wd_ref, wm_ref, x_ref, g1_ref, ng_ref, sc_ref, sh_ref, x1_ref, h2_ref):
    mix = (jnp.dot(d_ref[0], wd_ref[...], preferred_element_type=F32)
           + jnp.dot(m_ref[0], wm_ref[...], preferred_element_type=F32))
    x1 = x_ref[0] + g1_ref[0] * mix
    x1_ref[0] = x1
    h2_ref[0] = _rms_modulate(x1, ng_ref[...], sc_ref[0], sh_ref[0]).astype(BF16)


def _out_proj(d_out, m_out, w_out_bf16, x, g1, norm_g, sc, sh, tm=512):
    B, S, D = x.shape
    half = d_out.shape[-1]
    vec = pl.BlockSpec((1, 1, D), lambda b, i: (b, 0, 0))
    tok = lambda w: pl.BlockSpec((1, tm, w), lambda b, i: (b, i, 0))
    return pl.pallas_call(
        _out_kernel,
        grid=(B, S // tm),
        in_specs=[tok(half), tok(half),
                  pl.BlockSpec((half, D), lambda b, i: (0, 0)),
                  pl.BlockSpec((half, D), lambda b, i: (1, 0)),
                  tok(D), vec,
                  pl.BlockSpec((1, D), lambda b, i: (0, 0)),
                  vec, vec],
        out_specs=[tok(D), tok(D)],
        out_shape=[jax.ShapeDtypeStruct((B, S, D), F32), jax.ShapeDtypeStruct((B, S, D), BF16)],
        compiler_params=_params(("parallel", "arbitrary")),
        name="out_proj",
    )(d_out, m_out, w_out_bf16, w_out_bf16, x, g1.reshape(B, 1, D), norm_g.reshape(1, D),
      sc.reshape(B, 1, D), sh.reshape(B, 1, D))


def _sort_network(n):
    def merge(lo, hi, r):
        step = 2 * r
        if step < hi - lo:
            yield from merge(lo, hi, step)
            yield from merge(lo + r, hi, step)
            yield from ((i, i + r) for i in range(lo + r, hi - r, step))
        else:
            yield (lo, lo + r)

    def sort(lo, hi):
        if hi - lo >= 1:
            mid = lo + (hi - lo) // 2
            yield from sort(lo, mid)
            yield from sort(mid + 1, hi)
            yield from merge(lo, hi, 1)

    return list(sort(0, n - 1))


def _top_values(blocks, n):
    blocks = list(blocks) + [None] * (n - len(blocks))
    for i, j in _sort_network(n):
        a, b = blocks[i], blocks[j]
        if b is None:
            continue
        blocks[i], blocks[j] = (b, None) if a is None else (jnp.maximum(a, b), jnp.minimum(a, b))
    cols = blocks[0].shape[1]
    row = lax.broadcasted_iota(jnp.int32, (n, cols), 0)
    out = jnp.zeros((n, cols), F32)
    for r in range(n):
        m = jnp.max(blocks[0], axis=0, keepdims=True)
        out = jnp.where(row == r, m, out)
        hit = blocks[0] == m
        for d in range(n - 1 - r):
            if blocks[d] is not None:
                nxt = -jnp.inf if blocks[d + 1] is None else blocks[d + 1]
                blocks[d] = jnp.where(hit, nxt, blocks[d])
    return out


def _rank_among(v, s):
    def vrow(b):
        return v[b:b + 1]
    c8 = vrow(7) > s
    c4 = jnp.where(c8, vrow(11), vrow(3)) > s
    c2 = jnp.where(c8, jnp.where(c4, vrow(13), vrow(9)), jnp.where(c4, vrow(5), vrow(1))) > s
    hi = jnp.where(c4, jnp.where(c2, vrow(14), vrow(12)), jnp.where(c2, vrow(10), vrow(8)))
    lo = jnp.where(c4, jnp.where(c2, vrow(6), vrow(4)), jnp.where(c2, vrow(2), vrow(0)))
    c1 = jnp.where(c8, hi, lo) > s
    rank = ((jnp.where(c8, 8.0, 0.0) + jnp.where(c4, 4.0, 0.0))
            + (jnp.where(c2, 2.0, 0.0) + jnp.where(c1, 1.0, 0.0)))
    return jnp.where(vrow(15) > s, 16.0, rank)


def _gelu_exact(a):
    return 0.5 * a * (1.0 + lax.erf(a * math.sqrt(0.5)))


def _peer_kernel(h2_ref, x1_ref, g2_ref, wqt_ref, keys_ref, u_ref, vt_ref, fg_ref, o_ref,
                 h2t_s, e1_s, n_s, e2_s, r2_s, act_s, p_s, acc_s, *, tm, ce, n_chunks):
    s = pl.program_id(1)
    nk = PEER_NKEYS

    @pl.when(s == 0)
    def _select():
        h2t_s[...] = h2_ref[...].astype(F32).T.astype(BF16)
        qt = jnp.dot(wqt_ref[...], h2t_s[...], preferred_element_type=F32)
        tl = 256
        for h in range(PEER_HEADS):
            sc = []
            for c in range(2):
                hc = 2 * h + c
                q = qt[hc * PEER_KEY_DIM:(hc + 1) * PEER_KEY_DIM].astype(BF16)
                sc.append(jnp.dot(keys_ref[hc], q, preferred_element_type=F32))
            for t0 in range(0, tm, tl):
                cols = slice(t0, t0 + tl)
                s1, s2 = sc[0][:, cols], sc[1][:, cols]
                k = PEER_TOPK
                v1 = _top_values([s1[8 * r:8 * r + 8] for r in range(nk // 8)], k)
                v2 = _top_values([s2[8 * r:8 * r + 8] for r in range(nk // 8)], k)
                v2a, v2b = v2[0:8], v2[8:16]
                row8 = lax.broadcasted_iota(jnp.int32, (8, tl), 0)
                cand = [v1[0:1] + v2a, v1[0:1] + v2b, v1[1:2] + v2a]
                for a in range(2, 8):
                    cand.append(jnp.where(row8 < k // (a + 1), v1[a:a + 1] + v2a, -jnp.inf))
                cand.append(v1[8:16] + v2[0:1])
                tau = _top_values(cand, k)[k - 1:k]
                peak = v1[0:1] + v2[0:1]
                z = jnp.zeros((1, tl), F32)
                count = []
                for c in cand:
                    keep = c >= tau
                    z = z + jnp.sum(jnp.where(keep, jnp.exp(c - peak), 0.0), axis=0, keepdims=True)
                    count.append(jnp.sum(jnp.where(keep, 1.0, 0.0), axis=0, keepdims=True))
                n_by_rank = [count[0] + count[1]] + count[2:9]
                n_sel = jnp.zeros((nk, tl), F32)
                for a in range(8):
                    n_sel = jnp.where(s1 == v1[a:a + 1], n_by_rank[a], n_sel)
                low = (s1 < v1[7:8]) & (s1 >= v1[k - 1:k]) & ((s1 + v2[0:1]) >= tau)
                n_sel = jnp.where(low, 1.0, n_sel)
                rank2 = _rank_among(v2, s2)
                n_s[h, :, cols] = n_sel
                r2_s[h, :, cols] = rank2.astype(BF16)
                e1_s[h, :, cols] = jnp.where(s1 >= v1[k - 1:k], jnp.exp(s1 - v1[0:1]), 0.0) / z
                e2_s[h, :, cols] = jnp.where(s2 >= v2[k - 1:k], jnp.exp(s2 - v2[0:1]), 0.0).astype(BF16)
        acc_s[...] = jnp.zeros(acc_s.shape, F32)

    act_s[...] = jnp.dot(u_ref[...], h2t_s[...], preferred_element_type=F32)
    for ii in range(ce // nk):
        i = s * (ce // nk) + ii
        g = jnp.zeros((nk, tm), BF16)
        for h in range(PEER_HEADS):
            n_row = n_s[h, pl.ds(i, 1), :].astype(BF16)
            e1_row = e1_s[h, pl.ds(i, 1), :].astype(BF16)
            e2 = e2_s[h]
            g = g + e1_row * jnp.where(r2_s[h] < n_row, e2, jnp.zeros_like(e2))
        rows = slice(ii * nk, (ii + 1) * nk)
        p_s[rows] = (g.astype(F32) * _gelu_exact(act_s[rows])).astype(BF16)
    acc_s[...] += jnp.dot(vt_ref[0], p_s[...], preferred_element_type=F32)

    @pl.when(s == pl.num_programs(1) - 1)
    def _fin():
        y = x1_ref[...] + g2_ref[0] * acc_s[...].T
        ms = jnp.mean(y * y, axis=-1, keepdims=True)
        o_ref[...] = y * lax.rsqrt(ms + NORM_EPS) * fg_ref[...]


def _peer(h2, x1, g2, wqt, keys, u, vt, final_g, S, tm=512, ce=512):
    N, D = h2.shape
    n_chunks = u.shape[0] // ce
    vt = vt.reshape(D, n_chunks, ce).transpose(1, 0, 2)
    B = g2.shape[0]
    kern = functools.partial(_peer_kernel, tm=tm, ce=ce, n_chunks=n_chunks)
    head_rows = pltpu.VMEM((PEER_HEADS, PEER_NKEYS, tm), F32)
    head_rows_bf16 = pltpu.VMEM((PEER_HEADS, PEER_NKEYS, tm), BF16)
    return pl.pallas_call(
        kern,
        grid=(N // tm, n_chunks),
        in_specs=[pl.BlockSpec((tm, D), lambda t, s: (t, 0)),
                  pl.BlockSpec((tm, D), lambda t, s: (t, 0)),
                  pl.BlockSpec((1, 1, D), lambda t, s: ((t * tm) // S, 0, 0)),
                  pl.BlockSpec(wqt.shape, lambda t, s: (0, 0)),
                  pl.BlockSpec(keys.shape, lambda t, s: (0, 0, 0)),
                  pl.BlockSpec((ce, D), lambda t, s: (s, 0)),
                  pl.BlockSpec((1, D, ce), lambda t, s: (s, 0, 0)),
                  pl.BlockSpec((1, D), lambda t, s: (0, 0))],
        out_specs=pl.BlockSpec((tm, D), lambda t, s: (t, 0)),
        out_shape=jax.ShapeDtypeStruct((N, D), F32),
        scratch_shapes=[pltpu.VMEM((D, tm), BF16),
                        head_rows, head_rows, head_rows_bf16, head_rows_bf16,
                        pltpu.VMEM((ce, tm), F32),
                        pltpu.VMEM((ce, tm), BF16),
                        pltpu.VMEM((D, tm), F32)],
        compiler_params=_params(("parallel", "arbitrary")),
        name="peer",
    )(h2, x1, g2.reshape(B, 1, D), wqt, keys, u, vt, final_g.reshape(1, D))


def kernel(x, c, rel_bias, w_ada, b_ada, norm1_g, w_in, diff_lambda, diff_subln_g, w_out, norm2_g,
           peer_wq, peer_keys, peer_u, peer_v, final_g):
    B, S, D = x.shape
    assert w_ada.shape[0] == 1, "single-layer kernel"
    assert S % MOBA_BLOCK == 0 and D == D_MODEL
    diff_tiles, moba_tiles = _bias_tiles(rel_bias, S, MOBA_BLOCK)

    mod = _modulation(c, w_ada[0], b_ada[0])
    sh1, sc1, g1, sh2, sc2, g2 = jnp.split(mod, 6, axis=-1)

    w = w_in[0].astype(BF16)
    w_qv = jnp.concatenate([w[:, 0:512], w[:, 1024:2048], w[:, 2560:3072]], axis=1)
    w_k = jnp.concatenate([w[:, 512:1024], w[:, 2048:2560]], axis=1)
    qv, kt, kmean = _in_proj(x, norm1_g[0], sc1, sh1, w_qv, w_k)
    km = _pad_kmean(kmean.reshape(B, S // MOBA_BLOCK, 512))

    d_out = _attention(qv, kt, diff_tiles, (diff_lambda[0], diff_subln_g[0]), moba=False)
    m_out = _attention(qv, kt, moba_tiles, (km,), moba=True)

    x1, h2 = _out_proj(d_out, m_out, w_out[0].astype(BF16), x, g1, norm2_g[0], sc2, sh2)

    wqt = peer_wq[0].T.astype(BF16)
    keys = peer_keys[0].reshape(2 * PEER_HEADS, PEER_NKEYS, PEER_KEY_DIM).astype(BF16)
    out = _peer(h2.reshape(B * S, D), x1.reshape(B * S, D), g2, wqt, keys,
                peer_u[0].astype(BF16), peer_v[0].T.astype(BF16), final_g, S)
    return out.reshape(B, S, D)
```

```python
import functools
import math

import jax
import jax.numpy as jnp
from jax import lax
from jax.experimental import pallas as pl
from jax.experimental.pallas import tpu as pltpu

F32 = jnp.float32
BF16 = jnp.bfloat16

D_MODEL = 1024
DIFF_HEADS = 4
MOBA_HEADS = 8
HEAD_COLS = 128
HALF = 64
MOBA_BLOCK = 256
MOBA_TOPK = 3
QV_COLS = 2048
K_COLS = 1024
REL_BUCKETS = 32
REL_MAX_DIST = 128
PEER_HEADS = 8
PEER_NKEYS = 128
PEER_KEY_DIM = 128
PEER_TOPK = 16
NORM_EPS = 1e-6
NEG_INF = -1e30
LAMBDA_INIT = 0.8 - 0.6 * math.exp(-0.3 * 0)

VMEM_LIMIT_BYTES = 56 * 1024 * 1024


def _params(semantics):
    return pltpu.CompilerParams(dimension_semantics=semantics, vmem_limit_bytes=VMEM_LIMIT_BYTES)


def _mod_kernel(c_ref, w_ref, b_ref, o_ref):
    c = c_ref[...]
    act = c * jax.nn.sigmoid(c)
    o_ref[...] = jnp.dot(act, w_ref[...], preferred_element_type=F32,
                         precision=lax.Precision.HIGHEST) + b_ref[...]


def _modulation(c, w_ada, b_ada):
    B, D = c.shape
    n_out = w_ada.shape[1]
    tn = 1024
    return pl.pallas_call(
        _mod_kernel,
        grid=(n_out // tn,),
        in_specs=[pl.BlockSpec((B, D), lambda n: (0, 0)),
                  pl.BlockSpec((D, tn), lambda n: (0, n)),
                  pl.BlockSpec((1, tn), lambda n: (0, n))],
        out_specs=pl.BlockSpec((B, tn), lambda n: (0, n)),
        out_shape=jax.ShapeDtypeStruct((B, n_out), F32),
        compiler_params=_params(("arbitrary",)),
        name="mod",
    )(c, w_ada, b_ada.reshape(1, n_out))


def _rms_modulate(x, g, sc, sh):
    ms = jnp.mean(x * x, axis=-1, keepdims=True)
    return (x * lax.rsqrt(ms + NORM_EPS) * g) * (1.0 + sc) + sh


def _in_kernel(x_ref, g_ref, sc_ref, sh_ref, wqv_ref, wk_ref, qv_ref, kt_ref, kmean_ref, *, tm, cn):
    h = _rms_modulate(x_ref[0], g_ref[...], sc_ref[0], sh_ref[0]).astype(BF16)
    for n in range(QV_COLS // cn):
        cols = slice(n * cn, (n + 1) * cn)
        qv_ref[0, :, cols] = jnp.dot(h, wqv_ref[:, cols], preferred_element_type=F32).astype(BF16)
    for n in range(K_COLS // cn):
        cols = slice(n * cn, (n + 1) * cn)
        k = jnp.dot(h, wk_ref[:, cols], preferred_element_type=F32)
        kt_ref[0, cols, :] = k.T.astype(BF16)
        if n == 1:
            for r in range(tm // MOBA_BLOCK):
                kmean_ref[0, r] = jnp.mean(k[r * MOBA_BLOCK:(r + 1) * MOBA_BLOCK], axis=0, keepdims=True)


def _in_proj(x, g, sc, sh, w_qv, w_k, tm=512):
    B, S, D = x.shape
    cn = 512
    nblk = S // MOBA_BLOCK
    kern = functools.partial(_in_kernel, tm=tm, cn=cn)
    vec = pl.BlockSpec((1, 1, D), lambda b, i: (b, 0, 0))
    return pl.pallas_call(
        kern,
        grid=(B, S // tm),
        in_specs=[pl.BlockSpec((1, tm, D), lambda b, i: (b, i, 0)),
                  pl.BlockSpec((1, D), lambda b, i: (0, 0)),
                  vec, vec,
                  pl.BlockSpec((D, QV_COLS), lambda b, i: (0, 0)),
                  pl.BlockSpec((D, K_COLS), lambda b, i: (0, 0))],
        out_specs=[pl.BlockSpec((1, tm, QV_COLS), lambda b, i: (b, i, 0)),
                   pl.BlockSpec((1, K_COLS, tm), lambda b, i: (b, 0, i)),
                   pl.BlockSpec((1, tm // MOBA_BLOCK, 1, 512), lambda b, i: (b, i, 0, 0))],
        out_shape=[jax.ShapeDtypeStruct((B, S, QV_COLS), BF16),
                   jax.ShapeDtypeStruct((B, K_COLS, S), BF16),
                   jax.ShapeDtypeStruct((B, nblk, 1, 512), F32)],
        compiler_params=_params(("parallel", "arbitrary")),
        name="in_proj",
    )(x, g.reshape(1, D), sc.reshape(B, 1, D), sh.reshape(B, 1, D), w_qv, w_k)


def _attn_kernel(*refs, moba, tq, tk, nblk, nh):
    if moba:
        q_ref, kt_ref, v_ref, bias_ref, km_ref, o_ref, q2_s, s_s, p_s, m_s, l_s, acc_s = refs
    else:
        q_ref, kt_ref, v_ref, bias_ref, lam_ref, g_ref, o_ref, q2_s, s_s, p_s, m_s, l_s, acc_s = refs
    i = pl.program_id(2)
    lane = lax.broadcasted_iota(jnp.int32, (tq, HEAD_COLS), 1)
    reps = tk // HEAD_COLS

    def hcols(hb):
        return slice(hb * HEAD_COLS, (hb + 1) * HEAD_COLS)

    for hb in range(nh):
        qs = q_ref[0, :, hcols(hb)] * 0.125
        q_lo = jnp.where(lane < HALF, qs, jnp.zeros_like(qs))
        q_hi = jnp.where(lane >= HALF, qs, jnp.zeros_like(qs))
        q2_s[hb, 0:tq, 0:HEAD_COLS] = q_lo
        q2_s[hb, tq:2 * tq, 0:HEAD_COLS] = q_hi
        if moba:
            q2 = jnp.concatenate([q_lo, q_hi], axis=0).astype(F32)
            gate = lax.dot_general(km_ref[0, :, hcols(hb)], q2, (((1,), (1,)), ((), ())),
                                   preferred_element_type=F32,
                                   precision=lax.Precision.HIGHEST)
            nb8 = -(-nblk // 8) * 8
            gate = gate[0:nb8]
            blk = lax.broadcasted_iota(jnp.int32, (nb8, 2 * tq), 0)
            cnt = jnp.zeros((nb8, 2 * tq), jnp.int32)
            for jp in range(nblk):
                row = gate[jp:jp + 1]
                beats = (row > gate) | ((row == gate) & (blk > jp))
                cnt = cnt + jnp.where(beats, jnp.where(i > jp, 1, 0), 0)
            sel = ((blk < i) & (cnt < MOBA_TOPK)) | (blk == i)
            sb = jnp.where(sel, 0.0, NEG_INF).astype(F32)
            sb = jnp.concatenate([sb, jnp.zeros((HEAD_COLS - nb8, 2 * tq), F32)], axis=0)
            q2_s[hb, :, HEAD_COLS:2 * HEAD_COLS] = sb.T.astype(BF16)

    def tile_rows(j):
        return pl.ds(pl.multiple_of(j * tk, tk), tk)

    def qk(hb, j):
        kt = kt_ref[0, hcols(hb), tile_rows(j)]
        if moba:
            row_k = lax.broadcasted_iota(jnp.int32, (HEAD_COLS, tk), 0)
            kt = jnp.concatenate([kt, jnp.where(row_k == j, 1.0, 0.0).astype(BF16)], axis=0)
        return jnp.dot(q2_s[hb], kt, preferred_element_type=F32)

    def stage(j, kind, with_qk):
        cur = j % 2
        nxt = 1 - cur
        for hb in range(nh):
            pv = jnp.dot(p_s[2 * hb + nxt], v_ref[0, tile_rows(jnp.maximum(j - 1, 0)), hcols(hb)],
                         preferred_element_type=F32)
            s = s_s[2 * hb + cur]
            for m in range(2):
                hm = 2 * hb + m
                half = slice(m * tq, (m + 1) * tq)
                sm = s[half]
                if kind is not None:
                    sm = sm + bias_ref[hb, m, kind]
                m_prev = m_s[hm]
                m_new = jnp.maximum(m_prev, jnp.max(sm, axis=1, keepdims=True))
                alpha = jnp.exp(m_prev - m_new)
                p = jnp.exp(sm - jnp.concatenate([m_new] * reps, axis=1))
                psum = p[:, 0:HEAD_COLS]
                for r in range(1, reps):
                    psum = psum + p[:, r * HEAD_COLS:(r + 1) * HEAD_COLS]
                l_s[hm] = alpha * l_s[hm] + psum
                m_s[hm] = m_new
                acc_s[hm] = alpha * (acc_s[hm] + pv[half])
                p_s[2 * hb + cur, half] = p.astype(BF16)
            if with_qk:
                s_s[2 * hb + nxt] = qk(hb, j + 1)

    m_s[...] = jnp.full(m_s.shape, -jnp.inf, F32)
    l_s[...] = jnp.zeros(l_s.shape, F32)
    acc_s[...] = jnp.zeros(acc_s.shape, F32)
    for hb in range(nh):
        p_s[2 * hb + 1] = jnp.zeros(p_s.shape[1:], BF16)
        s_s[2 * hb] = qk(hb, 0)

    def far(j, carry):
        stage(j, None, True)
        return carry

    lax.fori_loop(0, jnp.maximum(i - 1, 0), far, 0)

    @pl.when(i >= 1)
    def _near():
        stage(i - 1, 1, True)

    stage(i, 0, False)
    for hb in range(nh):
        pv = jnp.dot(p_s[2 * hb + i % 2], v_ref[0, tile_rows(i), hcols(hb)], preferred_element_type=F32)
        o0 = (acc_s[2 * hb] + pv[0:tq]) / jnp.sum(l_s[2 * hb], axis=1, keepdims=True)
        o1 = (acc_s[2 * hb + 1] + pv[tq:2 * tq]) / jnp.sum(l_s[2 * hb + 1], axis=1, keepdims=True)
        if moba:
            o = jnp.where(lane < HALF, o0, o1)
        else:
            lp = lam_ref[...]
            lam = (jnp.exp(jnp.sum(lp[0:1] * lp[1:2], axis=1, keepdims=True))
                   - jnp.exp(jnp.sum(lp[2:3] * lp[3:4], axis=1, keepdims=True)) + LAMBDA_INIT)
            d = o0 - lam * o1
            ms = jnp.mean(d * d, axis=-1, keepdims=True)
            o = (d * lax.rsqrt(ms + NORM_EPS) * g_ref[...]) * (1.0 - LAMBDA_INIT)
        o_ref[0, :, hcols(hb)] = o.astype(BF16)


def _attention(qv, kt, bias_tiles, extra, *, moba, nh=1):
    B, S, _ = qv.shape
    col0, krow0 = (8, 4) if moba else (0, 0)
    tq = tk = MOBA_BLOCK
    nq = S // tq
    nblk = S // MOBA_BLOCK
    w = nh * HEAD_COLS
    kern = functools.partial(_attn_kernel, moba=moba, tq=tq, tk=tk, nblk=nblk, nh=nh)
    in_specs = [
        pl.BlockSpec((1, tq, w), lambda b, h, i: (b, i, col0 // nh + h)),
        pl.BlockSpec((1, w, S), lambda b, h, i: (b, krow0 // nh + h, 0)),
        pl.BlockSpec((1, S, w), lambda b, h, i: (b, 0, (col0 + 4) // nh + h)),
        pl.BlockSpec((nh, 2, 2, tq, tk), lambda b, h, i: (h, 0, 0, 0, 0)),
    ]
    if moba:
        (km,) = extra
        in_specs.append(pl.BlockSpec((1, HEAD_COLS, w), lambda b, h, i: (b, 0, h)))
        args = (qv, kt, qv, bias_tiles, km)
    else:
        lam_p, subln_g = extra
        in_specs.append(pl.BlockSpec(lam_p.shape, lambda b, h, i: (0, 0)))
        in_specs.append(pl.BlockSpec((1, HEAD_COLS), lambda b, h, i: (0, 0)))
        args = (qv, kt, qv, bias_tiles, lam_p, subln_g.reshape(1, HEAD_COLS))
    scratch = [pltpu.VMEM((nh, 2 * tq, 2 * HEAD_COLS if moba else HEAD_COLS), BF16),
               pltpu.VMEM((2 * nh, 2 * tq, tk), F32),
               pltpu.VMEM((2 * nh, 2 * tq, tk), BF16),
               pltpu.VMEM((2 * nh, tq, HEAD_COLS), F32),
               pltpu.VMEM((2 * nh, tq, HEAD_COLS), F32),
               pltpu.VMEM((2 * nh, tq, HEAD_COLS), F32)]
    return pl.pallas_call(
        kern,
        grid=(B, 4 // nh, nq),
        in_specs=in_specs,
        out_specs=pl.BlockSpec((1, tq, w), lambda b, h, i: (b, i, h)),
        out_shape=jax.ShapeDtypeStruct((B, S, 4 * HEAD_COLS), BF16),
        scratch_shapes=scratch,
        compiler_params=_params(("parallel", "parallel", "arbitrary")),
        name="moba_attn" if moba else "diff_attn",
    )(*args)


def _pad_kmean(kmean):
    return jnp.pad(kmean, ((0, 0), (0, HEAD_COLS - kmean.shape[1]), (0, 0)))


def _t5_bucket(n):
    max_exact = REL_BUCKETS // 2
    nf = jnp.maximum(n, 1).astype(F32)
    large = max_exact + (jnp.log(nf / max_exact) / math.log(REL_MAX_DIST / max_exact)
                         * (REL_BUCKETS - max_exact)).astype(jnp.int32)
    large = jnp.minimum(large, REL_BUCKETS - 1)
    return jnp.where(n < max_exact, n, large)


def _bias_tiles(rel_bias, S, t):
    assert REL_MAX_DIST <= t and 2 * t <= S
    by_dist = rel_bias[_t5_bucket(jnp.arange(S))] - rel_bias[REL_BUCKETS - 1]
    m = jnp.arange(2 * t)
    d = jnp.where(m < t, -m, 2 * t - m)
    tiles = []
    for kind in range(2):
        w = by_dist[jnp.clip(kind * t + d, 0, S - 1)]
        if kind == 0:
            w = jnp.where((d >= 0)[:, None], w, NEG_INF)
        skew = jnp.tile(w.T, (1, t))[:, :t * (2 * t - 1)].reshape(-1, t, 2 * t - 1)
        tiles.append(skew[:, :, :t])
    tiles = jnp.stack(tiles, axis=1)
    diff = tiles[:2 * DIFF_HEADS].reshape(2, DIFF_HEADS, 2, t, t).transpose(1, 0, 2, 3, 4)
    moba = tiles[2 * DIFF_HEADS:].reshape(MOBA_HEADS // 2, 2, 2, t, t)
    return diff, moba


def _out_kernel(d_ref, m_ref, wd_ref, wm_ref, x_ref, g1_ref, ng_ref, sc_ref, sh_ref, x1_ref, h2_ref):
    mix = (jnp.dot(d_ref[0], wd_ref[...], preferred_element_type=F32)
           + jnp.dot(m_ref[0], wm_ref[...], preferred_element_type=F32))
    x1 = x_ref[0] + g1_ref[0] * mix
    x1_ref[0] = x1
    h2_ref[0] = _rms_modulate(x1, ng_ref[...], sc_ref[0], sh_ref[0]).astype(BF16)


def _out_proj(d_out, m_out, w_out_bf16, x, g1, norm_g, sc, sh, tm=512):
    B, S, D = x.shape
    half = d_out.shape[-1]
    vec = pl.BlockSpec((1, 1, D), lambda b, i: (b, 0, 0))
    tok = lambda w: pl.BlockSpec((1, tm, w), lambda b, i: (b, i, 0))
    return pl.pallas_call(
        _out_kernel,
        grid=(B, S // tm),
        in_specs=[tok(half), tok(half),
                  pl.BlockSpec((half, D), lambda b, i: (0, 0)),
                  pl.BlockSpec((half, D), lambda b, i: (1, 0)),
                  tok(D), vec,
                  pl.BlockSpec((1, D), lambda b, i: (0, 0)),
                  vec, vec],
        out_specs=[tok(D), tok(D)],
        out_shape=[jax.ShapeDtypeStruct((B, S, D), F32), jax.ShapeDtypeStruct((B, S, D), BF16)],
        compiler_params=_params(("parallel", "arbitrary")),
        name="out_proj",
    )(d_out, m_out, w_out_bf16, w_out_bf16, x, g1.reshape(B, 1, D), norm_g.reshape(1, D),
      sc.reshape(B, 1, D), sh.reshape(B, 1, D))


def _sort_network(n):
    def merge(lo, hi, r):
        step = 2 * r
        if step < hi - lo:
            yield from merge(lo, hi, step)
            yield from merge(lo + r, hi, step)
            yield from ((i, i + r) for i in range(lo + r, hi - r, step))
        else:
            yield (lo, lo + r)

    def sort(lo, hi):
        if hi - lo >= 1:
            mid = lo + (hi - lo) // 2
            yield from sort(lo, mid)
            yield from sort(mid + 1, hi)
            yield from merge(lo, hi, 1)

    return list(sort(0, n - 1))


def _top_values(blocks, n):
    blocks = list(blocks) + [None] * (n - len(blocks))
    for i, j in _sort_network(n):
        a, b = blocks[i], blocks[j]
        if b is None:
            continue
        blocks[i], blocks[j] = (b, None) if a is None else (jnp.maximum(a, b), jnp.minimum(a, b))
    cols = blocks[0].shape[1]
    row = lax.broadcasted_iota(jnp.int32, (n, cols), 0)
    out = jnp.zeros((n, cols), F32)
    for r in range(n):
        m = jnp.max(blocks[0], axis=0, keepdims=True)
        out = jnp.where(row == r, m, out)
        hit = blocks[0] == m
        for d in range(n - 1 - r):
            if blocks[d] is not None:
                nxt = -jnp.inf if blocks[d + 1] is None else blocks[d + 1]
                blocks[d] = jnp.where(hit, nxt, blocks[d])
    return out


def _rank_among(v, s):
    def vrow(b):
        return v[b:b + 1]
    c8 = vrow(7) > s
    c4 = jnp.where(c8, vrow(11), vrow(3)) > s
    c2 = jnp.where(c8, jnp.where(c4, vrow(13), vrow(9)), jnp.where(c4, vrow(5), vrow(1))) > s
    hi = jnp.where(c4, jnp.where(c2, vrow(14), vrow(12)), jnp.where(c2, vrow(10), vrow(8)))
    lo = jnp.where(c4, jnp.where(c2, vrow(6), vrow(4)), jnp.where(c2, vrow(2), vrow(0)))
    c1 = jnp.where(c8, hi, lo) > s
    rank = ((jnp.where(c8, 8.0, 0.0) + jnp.where(c4, 4.0, 0.0))
            + (jnp.where(c2, 2.0, 0.0) + jnp.where(c1, 1.0, 0.0)))
    return jnp.where(vrow(15) > s, 16.0, rank)


def _gelu_exact(a):
    return 0.5 * a * (1.0 + lax.erf(a * math.sqrt(0.5)))


def _peer_kernel(h2_ref, x1_ref, g2_ref, wqt_ref, keys_ref, u_ref, vt_ref, fg_ref, o_ref,
                 h2t_s, e1_s, n_s, e2_s, r2_s, act_s, p_s, acc_s, *, tm, ce, n_chunks):
    s = pl.program_id(1)
    nk = PEER_NKEYS

    @pl.when(s == 0)
    def _select():
        h2t_s[...] = h2_ref[...].astype(F32).T.astype(BF16)
        qt = jnp.dot(wqt_ref[...], h2t_s[...], preferred_element_type=F32)
        tl = 256
        for h in range(PEER_HEADS):
            sc = []
            for c in range(2):
                hc = 2 * h + c
                q = qt[hc * PEER_KEY_DIM:(hc + 1) * PEER_KEY_DIM].astype(BF16)
                sc.append(jnp.dot(keys_ref[hc], q, preferred_element_type=F32))
            for t0 in range(0, tm, tl):
                cols = slice(t0, t0 + tl)
                s1, s2 = sc[0][:, cols], sc[1][:, cols]
                k = PEER_TOPK
                v1 = _top_values([s1[8 * r:8 * r + 8] for r in range(nk // 8)], k)
                v2 = _top_values([s2[8 * r:8 * r + 8] for r in range(nk // 8)], k)
                v2a, v2b = v2[0:8], v2[8:16]
                row8 = lax.broadcasted_iota(jnp.int32, (8, tl), 0)
                cand = [v1[0:1] + v2a, v1[0:1] + v2b, v1[1:2] + v2a]
                for a in range(2, 8):
                    cand.append(jnp.where(row8 < k // (a + 1), v1[a:a + 1] + v2a, -jnp.inf))
                cand.append(v1[8:16] + v2[0:1])
                tau = _top_values(cand, k)[k - 1:k]
                peak = v1[0:1] + v2[0:1]
                z = jnp.zeros((1, tl), F32)
                count = []
                for c in cand:
                    keep = c >= tau
                    z = z + jnp.sum(jnp.where(keep, jnp.exp(c - peak), 0.0), axis=0, keepdims=True)
                    count.append(jnp.sum(jnp.where(keep, 1.0, 0.0), axis=0, keepdims=True))
                n_by_rank = [count[0] + count[1]] + count[2:9]
                n_sel = jnp.zeros((nk, tl), F32)
                for a in range(8):
                    n_sel = jnp.where(s1 == v1[a:a + 1], n_by_rank[a], n_sel)
                low = (s1 < v1[7:8]) & (s1 >= v1[k - 1:k]) & ((s1 + v2[0:1]) >= tau)
                n_sel = jnp.where(low, 1.0, n_sel)
                rank2 = _rank_among(v2, s2)
                n_s[h, :, cols] = n_sel
                r2_s[h, :, cols] = rank2.astype(BF16)
                e1_s[h, :, cols] = jnp.where(s1 >= v1[k - 1:k], jnp.exp(s1 - v1[0:1]), 0.0) / z
                e2_s[h, :, cols] = jnp.where(s2 >= v2[k - 1:k], jnp.exp(s2 - v2[0:1]), 0.0).astype(BF16)
        acc_s[...] = jnp.zeros(acc_s.shape, F32)

    act_s[...] = jnp.dot(u_ref[...], h2t_s[...], preferred_element_type=F32)
    for ii in range(ce // nk):
        i = s * (ce // nk) + ii
        g = jnp.zeros((nk, tm), BF16)
        for h in range(PEER_HEADS):
            n_row = n_s[h, pl.ds(i, 1), :].astype(BF16)
            e1_row = e1_s[h, pl.ds(i, 1), :].astype(BF16)
            e2 = e2_s[h]
            g = g + e1_row * jnp.where(r2_s[h] < n_row, e2, jnp.zeros_like(e2))
        rows = slice(ii * nk, (ii + 1) * nk)
        p_s[rows] = (g.astype(F32) * _gelu_exact(act_s[rows])).astype(BF16)
    acc_s[...] += jnp.dot(vt_ref[0], p_s[...], preferred_element_type=F32)

    @pl.when(s == pl.num_programs(1) - 1)
    def _fin():
        y = x1_ref[...] + g2_ref[0] * acc_s[...].T
        ms = jnp.mean(y * y, axis=-1, keepdims=True)
        o_ref[...] = y * lax.rsqrt(ms + NORM_EPS) * fg_ref[...]


def _peer(h2, x1, g2, wqt, keys, u, vt, final_g, S, tm=512, ce=1024):
    N, D = h2.shape
    n_chunks = u.shape[0] // ce
    vt = vt.reshape(D, n_chunks, ce).transpose(1, 0, 2)
    B = g2.shape[0]
    kern = functools.partial(_peer_kernel, tm=tm, ce=ce, n_chunks=n_chunks)
    head_rows = pltpu.VMEM((PEER_HEADS, PEER_NKEYS, tm), F32)
    head_rows_bf16 = pltpu.VMEM((PEER_HEADS, PEER_NKEYS, tm), BF16)
    return pl.pallas_call(
        kern,
        grid=(N // tm, n_chunks),
        in_specs=[pl.BlockSpec((tm, D), lambda t, s: (t, 0)),
                  pl.BlockSpec((tm, D), lambda t, s: (t, 0)),
                  pl.BlockSpec((1, 1, D), lambda t, s: ((t * tm) // S, 0, 0)),
                  pl.BlockSpec(wqt.shape, lambda t, s: (0, 0)),
                  pl.BlockSpec(keys.shape, lambda t, s: (0, 0, 0)),
                  pl.BlockSpec((ce, D), lambda t, s: (s, 0)),
                  pl.BlockSpec((1, D, ce), lambda t, s: (s, 0, 0)),
                  pl.BlockSpec((1, D), lambda t, s: (0, 0))],
        out_specs=pl.BlockSpec((tm, D), lambda t, s: (t, 0)),
        out_shape=jax.ShapeDtypeStruct((N, D), F32),
        scratch_shapes=[pltpu.VMEM((D, tm), BF16),
                        head_rows, head_rows, head_rows_bf16, head_rows_bf16,
                        pltpu.VMEM((ce, tm), F32),
                        pltpu.VMEM((ce, tm), BF16),
                        pltpu.VMEM((D, tm), F32)],
        compiler_params=_params(("parallel", "arbitrary")),
        name="peer",
    )(h2, x1, g2.reshape(B, 1, D), wqt, keys, u, vt, final_g.reshape(1, D))


def kernel(x, c, rel_bias, w_ada, b_ada, norm1_g, w_in, diff_lambda, diff_subln_g, w_out, norm2_g,
           peer_wq, peer_keys, peer_u, peer_v, final_g):
    B, S, D = x.shape
    assert w_ada.shape[0] == 1, "single-layer kernel"
    assert S % MOBA_BLOCK == 0 and D == D_MODEL
    diff_tiles, moba_tiles = _bias_tiles(rel_bias, S, MOBA_BLOCK)

    mod = _modulation(c, w_ada[0], b_ada[0])
    sh1, sc1, g1, sh2, sc2, g2 = jnp.split(mod, 6, axis=-1)

    w = w_in[0].astype(BF16)
    w_qv = jnp.concatenate([w[:, 0:512], w[:, 1024:2048], w[:, 2560:3072]], axis=1)
    w_k = jnp.concatenate([w[:, 512:1024], w[:, 2048:2560]], axis=1)
    qv, kt, kmean = _in_proj(x, norm1_g[0], sc1, sh1, w_qv, w_k)
    km = _pad_kmean(kmean.reshape(B, S // MOBA_BLOCK, 512))

    d_out = _attention(qv, kt, diff_tiles, (diff_lambda[0], diff_subln_g[0]), moba=False)
    m_out = _attention(qv, kt, moba_tiles, (km,), moba=True)

    x1, h2 = _out_proj(d_out, m_out, w_out[0].astype(BF16), x, g1, norm2_g[0], sc2, sh2)

    wqt = peer_wq[0].T.astype(BF16)
    keys = peer_keys[0].reshape(2 * PEER_HEADS, PEER_NKEYS, PEER_KEY_DIM).astype(BF16)
    out = _peer(h2.reshape(B * S, D), x1.reshape(B * S, D), g2, wqt, keys,
                peer_u[0].astype(BF16), peer_v[0].T.astype(BF16), final_g, S)
    return out.reshape(B, S, D)
```

```python
import functools
import math

import jax
import jax.numpy as jnp
from jax import lax
from jax.experimental import pallas as pl
from jax.experimental.pallas import tpu as pltpu

F32 = jnp.float32
BF16 = jnp.bfloat16

D_MODEL = 1024
DIFF_HEADS = 4
MOBA_HEADS = 8
HEAD_COLS = 128
HALF = 64
MOBA_BLOCK = 256
MOBA_TOPK = 3
QV_COLS = 2048
K_COLS = 1024
REL_BUCKETS = 32
REL_MAX_DIST = 128
PEER_HEADS = 8
PEER_NKEYS = 128
PEER_KEY_DIM = 128
PEER_TOPK = 16
NORM_EPS = 1e-6
NEG_INF = -1e30
LAMBDA_INIT = 0.8 - 0.6 * math.exp(-0.3 * 0)

VMEM_LIMIT_BYTES = 56 * 1024 * 1024


def _params(semantics):
    return pltpu.CompilerParams(dimension_semantics=semantics, vmem_limit_bytes=VMEM_LIMIT_BYTES)


def _mod_kernel(c_ref, w_ref, b_ref, o_ref):
    c = c_ref[...]
    act = c * jax.nn.sigmoid(c)
    o_ref[...] = jnp.dot(act, w_ref[...], preferred_element_type=F32,
                         precision=lax.Precision.HIGHEST) + b_ref[...]


def _modulation(c, w_ada, b_ada):
    B, D = c.shape
    n_out = w_ada.shape[1]
    tn = 1024
    return pl.pallas_call(
        _mod_kernel,
        grid=(n_out // tn,),
        in_specs=[pl.BlockSpec((B, D), lambda n: (0, 0)),
                  pl.BlockSpec((D, tn), lambda n: (0, n)),
                  pl.BlockSpec((1, tn), lambda n: (0, n))],
        out_specs=pl.BlockSpec((B, tn), lambda n: (0, n)),
        out_shape=jax.ShapeDtypeStruct((B, n_out), F32),
        compiler_params=_params(("arbitrary",)),
        name="mod",
    )(c, w_ada, b_ada.reshape(1, n_out))


def _rms_modulate(x, g, sc, sh):
    ms = jnp.mean(x * x, axis=-1, keepdims=True)
    return (x * lax.rsqrt(ms + NORM_EPS) * g) * (1.0 + sc) + sh


def _in_kernel(x_ref, g_ref, sc_ref, sh_ref, wqv_ref, wk_ref, qv_ref, kt_ref, kmean_ref, *, tm, cn):
    h = _rms_modulate(x_ref[0], g_ref[...], sc_ref[0], sh_ref[0]).astype(BF16)
    for n in range(QV_COLS // cn):
        cols = slice(n * cn, (n + 1) * cn)
        qv_ref[0, :, cols] = jnp.dot(h, wqv_ref[:, cols], preferred_element_type=F32).astype(BF16)
    for n in range(K_COLS // cn):
        cols = slice(n * cn, (n + 1) * cn)
        k = jnp.dot(h, wk_ref[:, cols], preferred_element_type=F32)
        kt_ref[0, cols, :] = k.T.astype(BF16)
        if n == 1:
            for r in range(tm // MOBA_BLOCK):
                kmean_ref[0, r] = jnp.mean(k[r * MOBA_BLOCK:(r + 1) * MOBA_BLOCK], axis=0, keepdims=True)


def _in_proj(x, g, sc, sh, w_qv, w_k, tm=512):
    B, S, D = x.shape
    cn = 512
    nblk = S // MOBA_BLOCK
    kern = functools.partial(_in_kernel, tm=tm, cn=cn)
    vec = pl.BlockSpec((1, 1, D), lambda b, i: (b, 0, 0))
    return pl.pallas_call(
        kern,
        grid=(B, S // tm),
        in_specs=[pl.BlockSpec((1, tm, D), lambda b, i: (b, i, 0)),
                  pl.BlockSpec((1, D), lambda b, i: (0, 0)),
                  vec, vec,
                  pl.BlockSpec((D, QV_COLS), lambda b, i: (0, 0)),
                  pl.BlockSpec((D, K_COLS), lambda b, i: (0, 0))],
        out_specs=[pl.BlockSpec((1, tm, QV_COLS), lambda b, i: (b, i, 0)),
                   pl.BlockSpec((1, K_COLS, tm), lambda b, i: (b, 0, i)),
                   pl.BlockSpec((1, tm // MOBA_BLOCK, 1, 512), lambda b, i: (b, i, 0, 0))],
        out_shape=[jax.ShapeDtypeStruct((B, S, QV_COLS), BF16),
                   jax.ShapeDtypeStruct((B, K_COLS, S), BF16),
                   jax.ShapeDtypeStruct((B, nblk, 1, 512), F32)],
        compiler_params=_params(("parallel", "arbitrary")),
        name="in_proj",
    )(x, g.reshape(1, D), sc.reshape(B, 1, D), sh.reshape(B, 1, D), w_qv, w_k)


def _attn_kernel(*refs, moba, tq, tk, nblk, nh):
    if moba:
        q_ref, kt_ref, v_ref, bias_ref, km_ref, o_ref, q2_s, s_s, p_s, m_s, l_s, acc_s = refs
    else:
        q_ref, kt_ref, v_ref, bias_ref, lam_ref, g_ref, o_ref, q2_s, s_s, p_s, m_s, l_s, acc_s = refs
    i = pl.program_id(2)
    lane = lax.broadcasted_iota(jnp.int32, (tq, HEAD_COLS), 1)
    reps = tk // HEAD_COLS

    def hcols(hb):
        return slice(hb * HEAD_COLS, (hb + 1) * HEAD_COLS)

    for hb in range(nh):
        qs = q_ref[0, :, hcols(hb)] * 0.125
        q_lo = jnp.where(lane < HALF, qs, jnp.zeros_like(qs))
        q_hi = jnp.where(lane >= HALF, qs, jnp.zeros_like(qs))
        q2_s[hb, 0:tq, 0:HEAD_COLS] = q_lo
        q2_s[hb, tq:2 * tq, 0:HEAD_COLS] = q_hi
        if moba:
            q2 = jnp.concatenate([q_lo, q_hi], axis=0).astype(F32)
            gate = lax.dot_general(km_ref[0, :, hcols(hb)], q2, (((1,), (1,)), ((), ())),
                                   preferred_element_type=F32,
                                   precision=lax.Precision.HIGHEST)
            nb8 = -(-nblk // 8) * 8
            gate = gate[0:nb8]
            blk = lax.broadcasted_iota(jnp.int32, (nb8, 2 * tq), 0)
            cnt = jnp.zeros((nb8, 2 * tq), jnp.int32)
            for jp in range(nblk):
                row = gate[jp:jp + 1]
                beats = (row > gate) | ((row == gate) & (blk > jp))
                cnt = cnt + jnp.where(beats, jnp.where(i > jp, 1, 0), 0)
            sel = ((blk < i) & (cnt < MOBA_TOPK)) | (blk == i)
            sb = jnp.where(sel, 0.0, NEG_INF).astype(F32)
            sb = jnp.concatenate([sb, jnp.zeros((HEAD_COLS - nb8, 2 * tq), F32)], axis=0)
            q2_s[hb, :, HEAD_COLS:2 * HEAD_COLS] = sb.T.astype(BF16)

    def tile_rows(j):
        return pl.ds(pl.multiple_of(j * tk, tk), tk)

    def qk(hb, j):
        kt = kt_ref[0, hcols(hb), tile_rows(j)]
        if moba:
            row_k = lax.broadcasted_iota(jnp.int32, (HEAD_COLS, tk), 0)
            kt = jnp.concatenate([kt, jnp.where(row_k == j, 1.0, 0.0).astype(BF16)], axis=0)
        return jnp.dot(q2_s[hb], kt, preferred_element_type=F32)

    def stage(j, kind, with_qk, slot):
        cur = slot
        nxt = 1 - cur
        for hb in range(nh):
            pv = jnp.dot(p_s[2 * hb + nxt], v_ref[0, tile_rows(jnp.maximum(j - 1, 0)), hcols(hb)],
                         preferred_element_type=F32)
            s = s_s[2 * hb + cur]
            for m in range(2):
                hm = 2 * hb + m
                half = slice(m * tq, (m + 1) * tq)
                sm = s[half]
                if kind is not None:
                    sm = sm + bias_ref[hb, m, kind]
                m_prev = m_s[hm]
                m_new = jnp.maximum(m_prev, jnp.max(sm, axis=1, keepdims=True))
                alpha = jnp.exp(m_prev - m_new)
                p = jnp.exp(sm - jnp.concatenate([m_new] * reps, axis=1))
                psum = p[:, 0:HEAD_COLS]
                for r in range(1, reps):
                    psum = psum + p[:, r * HEAD_COLS:(r + 1) * HEAD_COLS]
                l_s[hm] = alpha * l_s[hm] + psum
                m_s[hm] = m_new
                acc_s[hm] = alpha * (acc_s[hm] + pv[half])
                p_s[2 * hb + cur, half] = p.astype(BF16)
            if with_qk:
                s_s[2 * hb + nxt] = qk(hb, j + 1)

    m_s[...] = jnp.full(m_s.shape, -jnp.inf, F32)
    l_s[...] = jnp.zeros(l_s.shape, F32)
    acc_s[...] = jnp.zeros(acc_s.shape, F32)
    for hb in range(nh):
        p_s[2 * hb + 1] = jnp.zeros(p_s.shape[1:], BF16)
        s_s[2 * hb] = qk(hb, 0)

    n_far = jnp.maximum(i - 1, 0)

    def far_pair(t, carry):
        stage(2 * t, None, True, slot=0)
        stage(2 * t + 1, None, True, slot=1)
        return carry

    lax.fori_loop(0, n_far // 2, far_pair, 0)

    def finish(slot):
        for hb in range(nh):
            pv = jnp.dot(p_s[2 * hb + slot], v_ref[0, tile_rows(i), hcols(hb)], preferred_element_type=F32)
            o0 = (acc_s[2 * hb] + pv[0:tq]) / jnp.sum(l_s[2 * hb], axis=1, keepdims=True)
            o1 = (acc_s[2 * hb + 1] + pv[tq:2 * tq]) / jnp.sum(l_s[2 * hb + 1], axis=1, keepdims=True)
            if moba:
                o = jnp.where(lane < HALF, o0, o1)
            else:
                lp = lam_ref[...]
                lam = (jnp.exp(jnp.sum(lp[0:1] * lp[1:2], axis=1, keepdims=True))
                       - jnp.exp(jnp.sum(lp[2:3] * lp[3:4], axis=1, keepdims=True)) + LAMBDA_INIT)
                d = o0 - lam * o1
                ms = jnp.mean(d * d, axis=-1, keepdims=True)
                o = (d * lax.rsqrt(ms + NORM_EPS) * g_ref[...]) * (1.0 - LAMBDA_INIT)
            o_ref[0, :, hcols(hb)] = o.astype(BF16)

    @pl.when(i == 0)
    def _first_tile():
        stage(i, 0, False, slot=0)
        finish(0)

    @pl.when((i >= 1) & (n_far % 2 == 0))
    def _even_tail():
        stage(i - 1, 1, True, slot=0)
        stage(i, 0, False, slot=1)
        finish(1)

    @pl.when(n_far % 2 == 1)
    def _odd_tail():
        stage(i - 2, None, True, slot=0)
        stage(i - 1, 1, True, slot=1)
        stage(i, 0, False, slot=0)
        finish(0)


def _attention(qv, kt, bias_tiles, extra, *, moba, nh=1):
    B, S, _ = qv.shape
    col0, krow0 = (8, 4) if moba else (0, 0)
    tq = tk = MOBA_BLOCK
    nq = S // tq
    nblk = S // MOBA_BLOCK
    w = nh * HEAD_COLS
    kern = functools.partial(_attn_kernel, moba=moba, tq=tq, tk=tk, nblk=nblk, nh=nh)
    in_specs = [
        pl.BlockSpec((1, tq, w), lambda b, h, i: (b, i, col0 // nh + h)),
        pl.BlockSpec((1, w, S), lambda b, h, i: (b, krow0 // nh + h, 0)),
        pl.BlockSpec((1, S, w), lambda b, h, i: (b, 0, (col0 + 4) // nh + h)),
        pl.BlockSpec((nh, 2, 2, tq, tk), lambda b, h, i: (h, 0, 0, 0, 0)),
    ]
    if moba:
        (km,) = extra
        in_specs.append(pl.BlockSpec((1, HEAD_COLS, w), lambda b, h, i: (b, 0, h)))
        args = (qv, kt, qv, bias_tiles, km)
    else:
        lam_p, subln_g = extra
        in_specs.append(pl.BlockSpec(lam_p.shape, lambda b, h, i: (0, 0)))
        in_specs.append(pl.BlockSpec((1, HEAD_COLS), lambda b, h, i: (0, 0)))
        args = (qv, kt, qv, bias_tiles, lam_p, subln_g.reshape(1, HEAD_COLS))
    scratch = [pltpu.VMEM((nh, 2 * tq, 2 * HEAD_COLS if moba else HEAD_COLS), BF16),
               pltpu.VMEM((2 * nh, 2 * tq, tk), F32),
               pltpu.VMEM((2 * nh, 2 * tq, tk), BF16),
               pltpu.VMEM((2 * nh, tq, HEAD_COLS), F32),
               pltpu.VMEM((2 * nh, tq, HEAD_COLS), F32),
               pltpu.VMEM((2 * nh, tq, HEAD_COLS), F32)]
    return pl.pallas_call(
        kern,
        grid=(B, 4 // nh, nq),
        in_specs=in_specs,
        out_specs=pl.BlockSpec((1, tq, w), lambda b, h, i: (b, i, h)),
        out_shape=jax.ShapeDtypeStruct((B, S, 4 * HEAD_COLS), BF16),
        scratch_shapes=scratch,
        compiler_params=_params(("parallel", "parallel", "arbitrary")),
        name="moba_attn" if moba else "diff_attn",
    )(*args)


def _pad_kmean(kmean):
    return jnp.pad(kmean, ((0, 0), (0, HEAD_COLS - kmean.shape[1]), (0, 0)))


def _t5_bucket(n):
    max_exact = REL_BUCKETS // 2
    nf = jnp.maximum(n, 1).astype(F32)
    large = max_exact + (jnp.log(nf / max_exact) / math.log(REL_MAX_DIST / max_exact)
                         * (REL_BUCKETS - max_exact)).astype(jnp.int32)
    large = jnp.minimum(large, REL_BUCKETS - 1)
    return jnp.where(n < max_exact, n, large)


def _bias_tiles(rel_bias, S, t):
    assert REL_MAX_DIST <= t and 2 * t <= S
    by_dist = rel_bias[_t5_bucket(jnp.arange(S))] - rel_bias[REL_BUCKETS - 1]
    m = jnp.arange(2 * t)
    d = jnp.where(m < t, -m, 2 * t - m)
    tiles = []
    for kind in range(2):
        w = by_dist[jnp.clip(kind * t + d, 0, S - 1)]
        if kind == 0:
            w = jnp.where((d >= 0)[:, None], w, NEG_INF)
        skew = jnp.tile(w.T, (1, t))[:, :t * (2 * t - 1)].reshape(-1, t, 2 * t - 1)
        tiles.append(skew[:, :, :t])
    tiles = jnp.stack(tiles, axis=1)
    diff = tiles[:2 * DIFF_HEADS].reshape(2, DIFF_HEADS, 2, t, t).transpose(1, 0, 2, 3, 4)
    moba = tiles[2 * DIFF_HEADS:].reshape(MOBA_HEADS // 2, 2, 2, t, t)
    return diff, moba


def _out_kernel(d_ref, m_ref, wd_ref, wm_ref, x_ref, g1_ref, ng_ref, sc_ref, sh_ref, x1_ref, h2_ref):
    mix = (jnp.dot(d_ref[0], wd_ref[...], preferred_element_type=F32)
           + jnp.dot(m_ref[0], wm_ref[...], preferred_element_type=F32))
    x1 = x_ref[0] + g1_ref[0] * mix
    x1_ref[0] = x1
    h2_ref[0] = _rms_modulate(x1, ng_ref[...], sc_ref[0], sh_ref[0]).astype(BF16)


def _out_proj(d_out, m_out, w_out_bf16, x, g1, norm_g, sc, sh, tm=512):
    B, S, D = x.shape
    half = d_out.shape[-1]
    vec = pl.BlockSpec((1, 1, D), lambda b, i: (b, 0, 0))
    tok = lambda w: pl.BlockSpec((1, tm, w), lambda b, i: (b, i, 0))
    return pl.pallas_call(
        _out_kernel,
        grid=(B, S // tm),
        in_specs=[tok(half), tok(half),
                  pl.BlockSpec((half, D), lambda b, i: (0, 0)),
                  pl.BlockSpec((half, D), lambda b, i: (1, 0)),
                  tok(D), vec,
                  pl.BlockSpec((1, D), lambda b, i: (0, 0)),
                  vec, vec],
        out_specs=[tok(D), tok(D)],
        out_shape=[jax.ShapeDtypeStruct((B, S, D), F32), jax.ShapeDtypeStruct((B, S, D), BF16)],
        compiler_params=_params(("parallel", "arbitrary")),
        name="out_proj",
    )(d_out, m_out, w_out_bf16, w_out_bf16, x, g1.reshape(B, 1, D), norm_g.reshape(1, D),
      sc.reshape(B, 1, D), sh.reshape(B, 1, D))


def _sort_network(n):
    def merge(lo, hi, r):
        step = 2 * r
        if step < hi - lo:
            yield from merge(lo, hi, step)
            yield from merge(lo + r, hi, step)
            yield from ((i, i + r) for i in range(lo + r, hi - r, step))
        else:
            yield (lo, lo + r)

    def sort(lo, hi):
        if hi - lo >= 1:
            mid = lo + (hi - lo) // 2
            yield from sort(lo, mid)
            yield from sort(mid + 1, hi)
            yield from merge(lo, hi, 1)

    return list(sort(0, n - 1))


def _top_values(blocks, n):
    blocks = list(blocks) + [None] * (n - len(blocks))
    for i, j in _sort_network(n):
        a, b = blocks[i], blocks[j]
        if b is None:
            continue
        blocks[i], blocks[j] = (b, None) if a is None else (jnp.maximum(a, b), jnp.minimum(a, b))
    cols = blocks[0].shape[1]
    row = lax.broadcasted_iota(jnp.int32, (n, cols), 0)
    out = jnp.zeros((n, cols), F32)
    for r in range(n):
        m = jnp.max(blocks[0], axis=0, keepdims=True)
        out = jnp.where(row == r, m, out)
        hit = blocks[0] == m
        for d in range(n - 1 - r):
            if blocks[d] is not None:
                nxt = -jnp.inf if blocks[d + 1] is None else blocks[d + 1]
                blocks[d] = jnp.where(hit, nxt, blocks[d])
    return out


def _rank_among(v, s):
    def vrow(b):
        return v[b:b + 1]
    c8 = vrow(7) > s
    c4 = jnp.where(c8, vrow(11), vrow(3)) > s
    c2 = jnp.where(c8, jnp.where(c4, vrow(13), vrow(9)), jnp.where(c4, vrow(5), vrow(1))) > s
    hi = jnp.where(c4, jnp.where(c2, vrow(14), vrow(12)), jnp.where(c2, vrow(10), vrow(8)))
    lo = jnp.where(c4, jnp.where(c2, vrow(6), vrow(4)), jnp.where(c2, vrow(2), vrow(0)))
    c1 = jnp.where(c8, hi, lo) > s
    rank = ((jnp.where(c8, 8.0, 0.0) + jnp.where(c4, 4.0, 0.0))
            + (jnp.where(c2, 2.0, 0.0) + jnp.where(c1, 1.0, 0.0)))
    return jnp.where(vrow(15) > s, 16.0, rank)


def _gelu_exact(a):
    return 0.5 * a * (1.0 + lax.erf(a * math.sqrt(0.5)))


def _peer_kernel(h2_ref, x1_ref, g2_ref, wqt_ref, keys_ref, u_ref, vt_ref, fg_ref, o_ref,
                 h2t_s, e1_s, n_s, e2_s, r2_s, act_s, p_s, acc_s, *, tm, ce, n_chunks):
    s = pl.program_id(1)
    nk = PEER_NKEYS

    @pl.when(s == 0)
    def _select():
        h2t_s[...] = h2_ref[...].astype(F32).T.astype(BF16)
        qt = jnp.dot(wqt_ref[...], h2t_s[...], preferred_element_type=F32)
        tl = 256
        for h in range(PEER_HEADS):
            sc = []
            for c in range(2):
                hc = 2 * h + c
                q = qt[hc * PEER_KEY_DIM:(hc + 1) * PEER_KEY_DIM].astype(BF16)
                sc.append(jnp.dot(keys_ref[hc], q, preferred_element_type=F32))
            for t0 in range(0, tm, tl):
                cols = slice(t0, t0 + tl)
                s1, s2 = sc[0][:, cols], sc[1][:, cols]
                k = PEER_TOPK
                v1 = _top_values([s1[8 * r:8 * r + 8] for r in range(nk // 8)], k)
                v2 = _top_values([s2[8 * r:8 * r + 8] for r in range(nk // 8)], k)
                v2a, v2b = v2[0:8], v2[8:16]
                row8 = lax.broadcasted_iota(jnp.int32, (8, tl), 0)
                cand = [v1[0:1] + v2a, v1[0:1] + v2b, v1[1:2] + v2a]
                for a in range(2, 8):
                    cand.append(jnp.where(row8 < k // (a + 1), v1[a:a + 1] + v2a, -jnp.inf))
                cand.append(v1[8:16] + v2[0:1])
                tau = _top_values(cand, k)[k - 1:k]
                peak = v1[0:1] + v2[0:1]
                z = jnp.zeros((1, tl), F32)
                count = []
                for c in cand:
                    keep = c >= tau
                    z = z + jnp.sum(jnp.where(keep, jnp.exp(c - peak), 0.0), axis=0, keepdims=True)
                    count.append(jnp.sum(jnp.where(keep, 1.0, 0.0), axis=0, keepdims=True))
                n_by_rank = [count[0] + count[1]] + count[2:9]
                n_sel = jnp.zeros((nk, tl), F32)
                for a in range(8):
                    n_sel = jnp.where(s1 == v1[a:a + 1], n_by_rank[a], n_sel)
                low = (s1 < v1[7:8]) & (s1 >= v1[k - 1:k]) & ((s1 + v2[0:1]) >= tau)
                n_sel = jnp.where(low, 1.0, n_sel)
                rank2 = _rank_among(v2, s2)
                n_s[h, :, cols] = n_sel
                r2_s[h, :, cols] = rank2.astype(BF16)
                e1_s[h, :, cols] = jnp.where(s1 >= v1[k - 1:k], jnp.exp(s1 - v1[0:1]), 0.0) / z
                e2_s[h, :, cols] = jnp.where(s2 >= v2[k - 1:k], jnp.exp(s2 - v2[0:1]), 0.0).astype(BF16)
        acc_s[...] = jnp.zeros(acc_s.shape, F32)

    act_s[...] = jnp.dot(u_ref[...], h2t_s[...], preferred_element_type=F32)
    for ii in range(ce // nk):
        i = s * (ce // nk) + ii
        g = jnp.zeros((nk, tm), BF16)
        for h in range(PEER_HEADS):
            n_row = n_s[h, pl.ds(i, 1), :].astype(BF16)
            e1_row = e1_s[h, pl.ds(i, 1), :].astype(BF16)
            e2 = e2_s[h]
            g = g + e1_row * jnp.where(r2_s[h] < n_row, e2, jnp.zeros_like(e2))
        rows = slice(ii * nk, (ii + 1) * nk)
        p_s[rows] = (g.astype(F32) * _gelu_exact(act_s[rows])).astype(BF16)
    acc_s[...] += jnp.dot(vt_ref[0], p_s[...], preferred_element_type=F32)

    @pl.when(s == pl.num_programs(1) - 1)
    def _fin():
        y = x1_ref[...] + g2_ref[0] * acc_s[...].T
        ms = jnp.mean(y * y, axis=-1, keepdims=True)
        o_ref[...] = y * lax.rsqrt(ms + NORM_EPS) * fg_ref[...]


def _peer(h2, x1, g2, wqt, keys, u, vt, final_g, S, tm=512, ce=2048):
    N, D = h2.shape
    n_chunks = u.shape[0] // ce
    vt = vt.reshape(D, n_chunks, ce).transpose(1, 0, 2)
    B = g2.shape[0]
    kern = functools.partial(_peer_kernel, tm=tm, ce=ce, n_chunks=n_chunks)
    head_rows = pltpu.VMEM((PEER_HEADS, PEER_NKEYS, tm), F32)
    head_rows_bf16 = pltpu.VMEM((PEER_HEADS, PEER_NKEYS, tm), BF16)
    return pl.pallas_call(
        kern,
        grid=(N // tm, n_chunks),
        in_specs=[pl.BlockSpec((tm, D), lambda t, s: (t, 0)),
                  pl.BlockSpec((tm, D), lambda t, s: (t, 0)),
                  pl.BlockSpec((1, 1, D), lambda t, s: ((t * tm) // S, 0, 0)),
                  pl.BlockSpec(wqt.shape, lambda t, s: (0, 0)),
                  pl.BlockSpec(keys.shape, lambda t, s: (0, 0, 0)),
                  pl.BlockSpec((ce, D), lambda t, s: (s, 0)),
                  pl.BlockSpec((1, D, ce), lambda t, s: (s, 0, 0)),
                  pl.BlockSpec((1, D), lambda t, s: (0, 0))],
        out_specs=pl.BlockSpec((tm, D), lambda t, s: (t, 0)),
        out_shape=jax.ShapeDtypeStruct((N, D), F32),
        scratch_shapes=[pltpu.VMEM((D, tm), BF16),
                        head_rows, head_rows, head_rows_bf16, head_rows_bf16,
                        pltpu.VMEM((ce, tm), F32),
                        pltpu.VMEM((ce, tm), BF16),
                        pltpu.VMEM((D, tm), F32)],
        compiler_params=_params(("parallel", "arbitrary")),
        name="peer",
    )(h2, x1, g2.reshape(B, 1, D), wqt, keys, u, vt, final_g.reshape(1, D))


def kernel(x, c, rel_bias, w_ada, b_ada, norm1_g, w_in, diff_lambda, diff_subln_g, w_out, norm2_g,
           peer_wq, peer_keys, peer_u, peer_v, final_g):
    B, S, D = x.shape
    assert w_ada.shape[0] == 1, "single-layer kernel"
    assert S % MOBA_BLOCK == 0 and D == D_MODEL
    diff_tiles, moba_tiles = _bias_tiles(rel_bias, S, MOBA_BLOCK)

    mod = _modulation(c, w_ada[0], b_ada[0])
    sh1, sc1, g1, sh2, sc2, g2 = jnp.split(mod, 6, axis=-1)

    w = w_in[0].astype(BF16)
    w_qv = jnp.concatenate([w[:, 0:512], w[:, 1024:2048], w[:, 2560:3072]], axis=1)
    w_k = jnp.concatenate([w[:, 512:1024], w[:, 2048:2560]], axis=1)
    qv, kt, kmean = _in_proj(x, norm1_g[0], sc1, sh1, w_qv, w_k)
    km = _pad_kmean(kmean.reshape(B, S // MOBA_BLOCK, 512))

    d_out = _attention(qv, kt, diff_tiles, (diff_lambda[0], diff_subln_g[0]), moba=False)
    m_out = _attention(qv, kt, moba_tiles, (km,), moba=True)

    x1, h2 = _out_proj(d_out, m_out, w_out[0].astype(BF16), x, g1, norm2_g[0], sc2, sh2)

    wqt = peer_wq[0].T.astype(BF16)
    keys = peer_keys[0].reshape(2 * PEER_HEADS, PEER_NKEYS, PEER_KEY_DIM).astype(BF16)
    out = _peer(h2.reshape(B * S, D), x1.reshape(B * S, D), g2, wqt, keys,
                peer_u[0].astype(BF16), peer_v[0].T.astype(BF16), final_g, S)
    return out.reshape(B, S, D)
```
